```python
import math
import jax, jax.numpy as jnp
from jax import lax
import numpy as np


D_MODEL = 2048
BATCH = 4
SEQ = 8192
DEPTH = 1

MLA_HEADS = 8
MLA_NOPE_DIM = 128
MLA_ROPE_DIM = 64
MLA_QK_DIM = MLA_NOPE_DIM + MLA_ROPE_DIM
MLA_V_DIM = 128
MLA_Q_LORA = 512
MLA_KV_LORA = 256
MLA_WIDTH = MLA_HEADS * MLA_V_DIM
GDN_HEADS = 8
GDN_K_DIM = 128
GDN_V_DIM = 128
GDN_WIDTH = GDN_HEADS * GDN_V_DIM
GDN_QKV = 2 * GDN_HEADS * GDN_K_DIM + GDN_WIDTH
CONV_WIDTH = 4
CHUNK = 64
D_MIX = MLA_WIDTH + GDN_WIDTH
ROPE_THETA = 10000.0
NORM_EPS = 1e-6
Q_BLOCK = 128
SPLIT_SIZES = (MLA_Q_LORA, MLA_KV_LORA, MLA_ROPE_DIM, MLA_WIDTH,
               GDN_HEADS * GDN_K_DIM, GDN_HEADS * GDN_K_DIM, GDN_WIDTH,
               GDN_HEADS, GDN_HEADS, GDN_WIDTH)
IN_COLS = sum(SPLIT_SIZES)

kernel_name = 'hymba_mla_gdn_gated_hybrid'


def rms_norm(x, gain):
    xf = x.astype(jnp.float32)
    y = xf * lax.rsqrt(jnp.mean(xf * xf, axis=-1, keepdims=True) + NORM_EPS)
    return (y * gain.astype(jnp.float32)).astype(x.dtype)


def l2_norm(x):
    xf = x.astype(jnp.float32)
    return xf * lax.rsqrt(jnp.sum(xf * xf, axis=-1, keepdims=True) + NORM_EPS)


def rope(x, positions):
    half = x.shape[-1] // 2
    inv_freq = jnp.power(ROPE_THETA, -jnp.arange(half, dtype=jnp.float32) / half)
    ang = positions.astype(jnp.float32)[..., None] * inv_freq
    cos = jnp.cos(ang)[:, :, None, :]
    sin = jnp.sin(ang)[:, :, None, :]
    xf = x.astype(jnp.float32)
    x1, x2 = xf[..., :half], xf[..., half:]
    return jnp.concatenate([x1 * cos - x2 * sin, x2 * cos + x1 * sin], axis=-1).astype(x.dtype)


def mla_branch(cq, ckv, k_rope, positions, q_a_gain, kv_a_gain, w_uq, w_ukv, q_gain, k_gain):
    B, S, _ = cq.shape
    q = (rms_norm(cq, q_a_gain) @ w_uq).reshape(B, S, MLA_HEADS, MLA_QK_DIM)
    kv = (rms_norm(ckv, kv_a_gain) @ w_ukv).reshape(B, S, MLA_HEADS, MLA_NOPE_DIM + MLA_V_DIM)
    k_nope, v = kv[..., :MLA_NOPE_DIM], kv[..., MLA_NOPE_DIM:]
    k_shared = jnp.broadcast_to(k_rope[:, :, None, :], (B, S, MLA_HEADS, MLA_ROPE_DIM))
    k = jnp.concatenate([k_nope, k_shared], axis=-1)
    q = rms_norm(q, q_gain)
    k = rms_norm(k, k_gain)
    q = jnp.concatenate([q[..., :MLA_NOPE_DIM], rope(q[..., MLA_NOPE_DIM:], positions)], axis=-1)
    k = jnp.concatenate([k[..., :MLA_NOPE_DIM], rope(k[..., MLA_NOPE_DIM:], positions)], axis=-1)
    q = q.transpose(0, 2, 1, 3)
    k = k.transpose(0, 2, 1, 3)
    v = v.transpose(0, 2, 1, 3)
    scale = MLA_QK_DIM ** -0.5
    key_idx = jnp.arange(S)

    def block(i):
        start = i * Q_BLOCK
        qb = lax.dynamic_slice_in_dim(q, start, Q_BLOCK, axis=2)
        s = jnp.einsum('bhqd,bhkd->bhqk', qb, k, preferred_element_type=jnp.float32) * scale
        q_idx = start + jnp.arange(Q_BLOCK)
        s = jnp.where(key_idx[None, :] <= q_idx[:, None], s, -jnp.inf)
        p = jax.nn.softmax(s, axis=-1).astype(v.dtype)
        return jnp.einsum('bhqk,bhkd->bqhd', p, v)

    o = lax.map(block, jnp.arange(S // Q_BLOCK))
    return o.transpose(1, 0, 2, 3, 4).reshape(B, S, MLA_WIDTH)


def causal_conv_silu(x, w):
    S = x.shape[1]
    xp = jnp.pad(x, ((0, 0), (CONV_WIDTH - 1, 0), (0, 0)))
    y = xp[:, 0:S, :] * w[0]
    for j in range(1, CONV_WIDTH):
        y = y + xp[:, j:j + S, :] * w[j]
    return jax.nn.silu(y)


def chunked_gated_delta(q, k, v, g, beta):
    B, S, H, Dk = q.shape
    Dv = v.shape[-1]
    N = S // CHUNK

    def to_chunks(t):
        return t.reshape(B, N, CHUNK, H, t.shape[-1]).transpose(0, 3, 1, 2, 4)

    q, k, v = to_chunks(q), to_chunks(k), to_chunks(v)
    g = g.reshape(B, N, CHUNK, H).transpose(0, 3, 1, 2)
    beta = beta.reshape(B, N, CHUNK, H).transpose(0, 3, 1, 2)
    gc = jnp.cumsum(g, axis=-1)
    idx = jnp.arange(CHUNK)
    lower_incl = idx[:, None] >= idx[None, :]
    strict = idx[:, None] > idx[None, :]
    diff = gc[..., :, None] - gc[..., None, :]
    decay = jnp.where(lower_incl, jnp.exp(jnp.where(lower_incl, diff, 0.0)), 0.0)
    k_beta = k * beta[..., None]
    v_beta = v * beta[..., None]
    L = jnp.where(strict, jnp.einsum('bhncd,bhnjd->bhncj', k_beta, k) * decay, 0.0)
    eye = jnp.eye(CHUNK, dtype=jnp.float32)
    rhs = jnp.concatenate([v_beta, k_beta * jnp.exp(gc)[..., None]], axis=-1)
    sol = lax.linalg.triangular_solve(eye + L, rhs, left_side=True, lower=True, unit_diagonal=True)
    u, w = sol[..., :Dv], sol[..., Dv:]
    attn_intra = jnp.einsum('bhncd,bhnjd->bhncj', q, k) * decay
    q_dec = q * jnp.exp(gc)[..., None]
    k_dec = k * jnp.exp(gc[..., -1:] - gc)[..., None]
    g_last = jnp.exp(gc[..., -1])

    def mv(t):
        return jnp.moveaxis(t, 2, 0)

    xs = (mv(u), mv(w), mv(q_dec), mv(k_dec), mv(attn_intra), jnp.moveaxis(g_last, 2, 0))

    def step(state, inp):
        u_c, w_c, qd, kd, a_c, gl = inp
        v_new = u_c - jnp.einsum('bhck,bhkv->bhcv', w_c, state)
        o = jnp.einsum('bhck,bhkv->bhcv', qd, state) + jnp.einsum('bhcj,bhjv->bhcv', a_c, v_new)
        state = state * gl[..., None, None] + jnp.einsum('bhck,bhcv->bhkv', kd, v_new)
        return state, o

    state0 = jnp.zeros((B, H, Dk, Dv), jnp.float32)
    _, o = lax.scan(step, state0, xs)
    return o.transpose(1, 0, 3, 2, 4).reshape(B, S, H, Dv)


def gdn_branch(gq, gk, gv, ga, gb, conv_w, a_log, dt_bias, out_gain):
    B, S, _ = gq.shape
    qkv = causal_conv_silu(jnp.concatenate([gq, gk, gv], axis=-1), conv_w)
    hk = GDN_HEADS * GDN_K_DIM
    q = qkv[..., :hk].reshape(B, S, GDN_HEADS, GDN_K_DIM)
    k = qkv[..., hk:2 * hk].reshape(B, S, GDN_HEADS, GDN_K_DIM)
    v = qkv[..., 2 * hk:].reshape(B, S, GDN_HEADS, GDN_V_DIM).astype(jnp.float32)
    q = l2_norm(q) * (GDN_K_DIM ** -0.5)
    k = l2_norm(k)
    g = -jnp.exp(a_log.astype(jnp.float32)) * jax.nn.softplus(ga.astype(jnp.float32) + dt_bias.astype(jnp.float32))
    beta = jax.nn.sigmoid(gb.astype(jnp.float32))
    o = chunked_gated_delta(q, k, v, g, beta)
    o = rms_norm(o, out_gain).astype(gq.dtype)
    return o.reshape(B, S, GDN_WIDTH)


def setup_inputs(seed: int = 0) -> dict:
    key = jax.random.key(seed)
    ks = jax.random.split(key, 16)
    f32 = jnp.float32

    def dense(k, shape, fan_in):
        return jax.random.normal(k, shape, f32) * fan_in ** -0.5

    def gain(k, shape):
        return 1.0 + 0.02 * jax.random.normal(k, shape, f32)

    x = jax.random.normal(ks[0], (BATCH, SEQ, D_MODEL), f32)
    positions = (jax.random.randint(ks[1], (BATCH, 1), 0, 1024, jnp.int32)
                 + jnp.arange(SEQ, dtype=jnp.int32)[None, :])
    dt = jnp.exp(jax.random.uniform(ks[11], (DEPTH, GDN_HEADS), f32,
                                    minval=math.log(1e-3), maxval=math.log(1e-1)))
    dt_bias = dt + jnp.log(-jnp.expm1(-dt))
    a_log = jnp.log(jax.random.uniform(ks[12], (DEPTH, GDN_HEADS), f32, minval=1.0, maxval=16.0))
    return {
        'x': x,
        'positions': positions,
        'norm_gain': gain(ks[2], (DEPTH, D_MODEL)),
        'w_in': dense(ks[3], (DEPTH, D_MODEL, IN_COLS), D_MODEL),
        'mla_q_a_gain': gain(ks[4], (DEPTH, MLA_Q_LORA)),
        'mla_kv_a_gain': gain(ks[5], (DEPTH, MLA_KV_LORA)),
        'w_uq': dense(ks[6], (DEPTH, MLA_Q_LORA, MLA_HEADS * MLA_QK_DIM), MLA_Q_LORA),
        'w_ukv': dense(ks[7], (DEPTH, MLA_KV_LORA, MLA_HEADS * (MLA_NOPE_DIM + MLA_V_DIM)), MLA_KV_LORA),
        'mla_q_norm_gain': gain(ks[8], (DEPTH, MLA_QK_DIM)),
        'mla_k_norm_gain': gain(ks[9], (DEPTH, MLA_QK_DIM)),
        'gdn_conv_w': dense(ks[10], (DEPTH, CONV_WIDTH, GDN_QKV), CONV_WIDTH),
        'gdn_a_log': a_log,
        'gdn_dt_bias': dt_bias,
        'gdn_out_norm_gain': gain(ks[13], (DEPTH, GDN_V_DIM)),
        'w_out': dense(ks[14], (DEPTH, D_MIX, D_MODEL), D_MIX),
    }


def reference(x, positions, norm_gain, w_in, mla_q_a_gain, mla_kv_a_gain, w_uq, w_ukv,
              mla_q_norm_gain, mla_k_norm_gain, gdn_conv_w, gdn_a_log, gdn_dt_bias,
              gdn_out_norm_gain, w_out):
    B, S, _ = x.shape
    split_idx = np.cumsum(SPLIT_SIZES)[:-1].tolist()
    h = x
    for layer in range(DEPTH):
        xn = rms_norm(h, norm_gain[layer])
        proj = xn @ w_in[layer]
        cq, ckv, k_rope, mla_gate, gq, gk, gv, ga, gb, gdn_gate = jnp.split(proj, split_idx, axis=-1)
        o_mla = mla_branch(cq, ckv, k_rope, positions, mla_q_a_gain[layer], mla_kv_a_gain[layer],
                           w_uq[layer], w_ukv[layer], mla_q_norm_gain[layer], mla_k_norm_gain[layer])
        o_mla = o_mla * jax.nn.silu(mla_gate)
        o_gdn = gdn_branch(gq, gk, gv, ga, gb, gdn_conv_w[layer], gdn_a_log[layer],
                           gdn_dt_bias[layer], gdn_out_norm_gain[layer])
        o_gdn = o_gdn * jax.nn.silu(gdn_gate)
        mixed = jnp.concatenate([o_mla, o_gdn], axis=-1)
        h = h + mixed @ w_out[layer]
    return h
```

```python
import functools
import math

import jax
import jax.numpy as jnp
from jax import lax
from jax.experimental import pallas as pl
from jax.experimental.pallas import tpu as pltpu

BF = jnp.bfloat16
F32 = jnp.float32

D_MODEL = 2048
MLA_HEADS = 8
NOPE = 128
ROPE = 64
QK_DIM = NOPE + ROPE
V_DIM = 128
Q_LORA = 512
KV_LORA = 256
GDN_HEADS = 8
GDN_DIM = 128
GDN_W = GDN_HEADS * GDN_DIM
CONV_W = 4
CHUNK = 64
ROPE_THETA = 10000.0
EPS = 1e-6
LOG2E = math.log2(math.e)

P_GQ, P_GK, P_GV, P_GG, P_MG = 0, 1024, 2048, 3072, 4096
P_CQ, P_CKV, P_KR, P_G = 5120, 5632, 5888, 6016
P_COLS = 6144

VMEM_LIMIT = 56 * 1024 * 1024


def _cparams(sem):
    return pltpu.CompilerParams(dimension_semantics=sem, vmem_limit_bytes=VMEM_LIMIT)


IN_TM, IN_TN = 1024, 768


def _inproj_kernel(x_ref, gain_ref, w_ref, p_ref, g_ref, xn_ref):
    j = pl.program_id(1)

    @pl.when(j == 0)
    def _():
        xf = x_ref[...]
        ms = jnp.mean(xf * xf, axis=-1, keepdims=True)
        xn_ref[...] = (xf * lax.rsqrt(ms + EPS) * gain_ref[...]).astype(BF)

    acc = jnp.dot(xn_ref[...], w_ref[...], preferred_element_type=F32)
    p_ref[...] = acc.astype(BF)

    @pl.when(j == pl.num_programs(1) - 1)
    def _():
        g_ref[...] = acc[:, IN_TN - 128:]


def _in_proj(x2, gain, w):
    n = x2.shape[0]
    return pl.pallas_call(
        _inproj_kernel,
        grid=(n // IN_TM, P_COLS // IN_TN),
        in_specs=[
            pl.BlockSpec((IN_TM, D_MODEL), lambda i, j: (i, 0)),
            pl.BlockSpec((1, D_MODEL), lambda i, j: (0, 0)),
            pl.BlockSpec((D_MODEL, IN_TN), lambda i, j: (0, j)),
        ],
        out_specs=[
            pl.BlockSpec((IN_TM, IN_TN), lambda i, j: (i, j)),
            pl.BlockSpec((IN_TM, 128), lambda i, j: (i, 0)),
        ],
        out_shape=[
            jax.ShapeDtypeStruct((n, P_COLS), BF),
            jax.ShapeDtypeStruct((n, 128), F32),
        ],
        scratch_shapes=[pltpu.VMEM((IN_TM, D_MODEL), BF)],
        compiler_params=_cparams(("parallel", "arbitrary")),
        name="in_proj",
    )(x2, gain, w)


PREP_TM = 512
HEAD_SLAB = 256


def _rms(x, gain):
    ms = jnp.mean(x * x, axis=-1, keepdims=True)
    return x * lax.rsqrt(ms + EPS) * gain


def _mla_prep_kernel(cq_ref, ckv_ref, kr_ref, pos_ref, qag_ref, kvag_ref, wuq_ref, wukv_ref,
                     qg_ref, kg_ref, invf_ref, q_ref, k_ref, v_ref):
    lane = lax.broadcasted_iota(jnp.int32, (1, 128), 1)
    lo64 = lane < ROPE

    ang = pos_ref[...].astype(F32) * invf_ref[...]
    cs = jnp.where(lo64, jnp.cos(ang), jnp.sin(ang))

    cqn = _rms(cq_ref[...].astype(F32), qag_ref[...]).astype(BF)
    y = jnp.dot(cqn, wuq_ref[...], preferred_element_type=F32)
    qg0 = qg_ref[:, :128]
    qg1 = qg_ref[:, 128:]
    qscale = QK_DIM ** -0.5 * LOG2E
    for h in range(MLA_HEADS):
        y0 = y[:, HEAD_SLAB * h: HEAD_SLAB * h + 128]
        y1 = y[:, HEAD_SLAB * h + 128: HEAD_SLAB * (h + 1)]
        ss = (jnp.sum(y0 * y0, axis=-1, keepdims=True)
              + jnp.sum(jnp.where(lo64, y1 * y1, 0.0), axis=-1, keepdims=True))
        r = lax.rsqrt(ss * (1.0 / QK_DIM) + EPS)
        qn = y0 * r * qg0
        z = y1 * r * qg1 * cs
        qr = z + pltpu.roll(z, ROPE, axis=1)
        q_ref[:, HEAD_SLAB * h: HEAD_SLAB * h + 128] = (qn * qscale).astype(BF)
        q_ref[:, HEAD_SLAB * h + 128: HEAD_SLAB * (h + 1)] = (qr * qscale).astype(BF)

    ckvn = _rms(ckv_ref[...].astype(F32), kvag_ref[...]).astype(BF)
    kv = jnp.dot(ckvn, wukv_ref[...], preferred_element_type=F32)
    kr = kr_ref[...].astype(F32)
    ssr = jnp.sum(jnp.where(lo64, kr * kr, 0.0), axis=-1, keepdims=True)
    kg0 = kg_ref[:, :128]
    kg1 = kg_ref[:, 128:]
    zk0 = kr * kg1 * cs
    for h in range(MLA_HEADS):
        kn = kv[:, HEAD_SLAB * h: HEAD_SLAB * h + 128]
        ss = jnp.sum(kn * kn, axis=-1, keepdims=True) + ssr
        r = lax.rsqrt(ss * (1.0 / QK_DIM) + EPS)
        zk = zk0 * r
        krh = jnp.where(lo64, zk + pltpu.roll(zk, ROPE, axis=1), 0.0)
        k_ref[:, HEAD_SLAB * h: HEAD_SLAB * h + 128] = (kn * r * kg0).astype(BF)
        k_ref[:, HEAD_SLAB * h + 128: HEAD_SLAB * (h + 1)] = krh.astype(BF)
        v_ref[:, V_DIM * h: V_DIM * (h + 1)] = kv[:, HEAD_SLAB * h + 128: HEAD_SLAB * (h + 1)].astype(BF)


def _mla_prep(p, pos, qag, kvag, wuq, wukv, qg, kg, invf):
    n = p.shape[0]
    tm = PREP_TM
    full = lambda r, c: pl.BlockSpec((r, c), lambda i: (0, 0))
    return pl.pallas_call(
        _mla_prep_kernel,
        grid=(n // tm,),
        in_specs=[
            pl.BlockSpec((tm, Q_LORA), lambda i: (i, P_CQ // Q_LORA)),
            pl.BlockSpec((tm, KV_LORA), lambda i: (i, P_CKV // KV_LORA)),
            pl.BlockSpec((tm, 128), lambda i: (i, P_KR // 128)),
            pl.BlockSpec((tm, 1), lambda i: (i, 0)),
            full(1, Q_LORA), full(1, KV_LORA),
            full(Q_LORA, MLA_HEADS * HEAD_SLAB), full(KV_LORA, MLA_HEADS * HEAD_SLAB),
            full(1, HEAD_SLAB), full(1, HEAD_SLAB), full(1, 128),
        ],
        out_specs=[
            pl.BlockSpec((tm, MLA_HEADS * HEAD_SLAB), lambda i: (i, 0)),
            pl.BlockSpec((tm, MLA_HEADS * HEAD_SLAB), lambda i: (i, 0)),
            pl.BlockSpec((tm, MLA_HEADS * V_DIM), lambda i: (i, 0)),
        ],
        out_shape=[
            jax.ShapeDtypeStruct((n, MLA_HEADS * HEAD_SLAB), BF),
            jax.ShapeDtypeStruct((n, MLA_HEADS * HEAD_SLAB), BF),
            jax.ShapeDtypeStruct((n, MLA_HEADS * V_DIM), BF),
        ],
        compiler_params=_cparams(("parallel",)),
        name="mla_prep",
    )(p, p, p, pos, qag, kvag, wuq, wukv, qg, kg, invf)


ATT_TQ = 512
ATT_TK = 512
NEG = -1e30


def _attn_kernel(q_ref, k_ref, v_ref, gate_ref, o_ref, m_ref, l_ref, acc_ref):
    qi = pl.program_id(2)
    q = q_ref[...]
    m_ref[...] = jnp.full(m_ref.shape, NEG, F32)
    l_ref[...] = jnp.zeros(l_ref.shape, F32)
    acc_ref[...] = jnp.zeros(acc_ref.shape, F32)

    def step(j, masked):
        off = pl.multiple_of(j * ATT_TK, ATT_TK)
        kj = k_ref[pl.ds(off, ATT_TK), :]
        vj = v_ref[pl.ds(off, ATT_TK), :]
        s = lax.dot_general(q, kj, (((1,), (1,)), ((), ())), preferred_element_type=F32)
        if masked:
            row = lax.broadcasted_iota(jnp.int32, s.shape, 0)
            col = lax.broadcasted_iota(jnp.int32, s.shape, 1)
            s = jnp.where(col <= row, s, NEG)
        m_prev = m_ref[...]
        m_new = jnp.maximum(m_prev, jnp.max(s, axis=-1, keepdims=True))
        alpha = jnp.exp2(m_prev - m_new)
        p = jnp.exp2(s - m_new)
        l_ref[...] = alpha * l_ref[...] + jnp.sum(p, axis=-1, keepdims=True)
        acc_ref[...] = alpha * acc_ref[...] + jnp.dot(p.astype(BF), vj, preferred_element_type=F32)
        m_ref[...] = m_new

    def body(j, carry):
        step(j, False)
        return carry

    lax.fori_loop(0, qi, body, 0)
    step(qi, True)

    g = gate_ref[...].astype(F32)
    o = acc_ref[...] * (1.0 / l_ref[...])
    o_ref[...] = (o * (g * jax.nn.sigmoid(g))).astype(BF)


def _mla_attn(qp, kp, vp, p, batch, seq):
    n = qp.shape[0]
    nq = seq // ATT_TQ
    return pl.pallas_call(
        _attn_kernel,
        grid=(batch, MLA_HEADS, nq),
        in_specs=[
            pl.BlockSpec((ATT_TQ, HEAD_SLAB), lambda b, h, i: (b * nq + i, h)),
            pl.BlockSpec((seq, HEAD_SLAB), lambda b, h, i: (b, h)),
            pl.BlockSpec((seq, V_DIM), lambda b, h, i: (b, h)),
            pl.BlockSpec((ATT_TQ, V_DIM), lambda b, h, i: (b * nq + i, P_MG // V_DIM + h)),
        ],
        out_specs=pl.BlockSpec((ATT_TQ, V_DIM), lambda b, h, i: (b * nq + i, h)),
        out_shape=jax.ShapeDtypeStruct((n, MLA_HEADS * V_DIM), BF),
        scratch_shapes=[
            pltpu.VMEM((ATT_TQ, 1), F32),
            pltpu.VMEM((ATT_TQ, 1), F32),
            pltpu.VMEM((ATT_TQ, V_DIM), F32),
        ],
        compiler_params=_cparams(("parallel", "parallel", "arbitrary")),
        name="mla_attn",
    )(qp, kp, vp, p)


GDN_T = 128
GDN_NC = GDN_T // CHUNK


def _split3(x):
    a = x.astype(BF)
    r1 = x - a.astype(F32)
    b = r1.astype(BF)
    c = (r1 - b.astype(F32)).astype(BF)
    return a, b, c


def _dot_nt(a, b):
    return lax.dot_general(a, b, (((1,), (1,)), ((), ())), preferred_element_type=F32)


def _dot_tn(a, b):
    return lax.dot_general(a, b, (((0,), (0,)), ((), ())), preferred_element_type=F32)


def _gdn_kernel(q_ref, k_ref, v_ref, hq_ref, hk_ref, hv_ref, g_ref, gate_ref,
                cw_ref, arow_ref, dtrow_ref, og_ref, o_ref,
                state_ref, xq_ref, xk_ref, xv_ref):
    t = pl.program_id(1)
    T = GDN_T

    @pl.when(t == 0)
    def _():
        state_ref[...] = jnp.zeros(state_ref.shape, F32)

    first = t == 0

    def conv_silu(x_ref, h_ref, xp_ref, col0):
        halo = jnp.where(first, 0.0, h_ref[...].astype(F32))
        xp_ref[0:8, :] = halo
        xp_ref[8:8 + T, :] = x_ref[...].astype(F32)
        y = xp_ref[pl.ds(5, T), :] * cw_ref[0:1, col0:col0 + GDN_W]
        for j in range(1, CONV_W):
            y = y + xp_ref[pl.ds(5 + j, T), :] * cw_ref[j:j + 1, col0:col0 + GDN_W]
        return y * jax.nn.sigmoid(y)

    qc = conv_silu(q_ref, hq_ref, xq_ref, 0)
    kc = conv_silu(k_ref, hk_ref, xk_ref, GDN_W)
    vc = conv_silu(v_ref, hv_ref, xv_ref, 2 * GDN_W)

    gin = g_ref[...]
    gx = gin + dtrow_ref[...]
    softplus = jnp.maximum(gx, 0.0) + jnp.log1p(jnp.exp(-jnp.abs(gx)))
    gdec = -jnp.exp(arow_ref[...]) * softplus
    beta = jax.nn.sigmoid(gin)

    ri = lax.broadcasted_iota(jnp.int32, (T, T), 0)
    ci = lax.broadcasted_iota(jnp.int32, (T, T), 1)
    same_chunk = jnp.right_shift(ri, 6) == jnp.right_shift(ci, 6)
    tri = jnp.where(same_chunk, jnp.where(ci <= ri, 1.0, 0.0), 0.0).astype(BF)
    g1, g2, g3 = _split3(gdec)
    gc = (jnp.dot(tri, g1, preferred_element_type=F32)
          + jnp.dot(tri, g2, preferred_element_type=F32)
          + jnp.dot(tri, g3, preferred_element_type=F32))

    er = lax.broadcasted_iota(jnp.int32, (128, GDN_W), 0)
    ec = lax.broadcasted_iota(jnp.int32, (128, GDN_W), 1)
    e_g = jnp.where(er == jnp.right_shift(ec, 7), 1.0, 0.0).astype(BF)
    e_b = jnp.where(er == jnp.right_shift(ec, 7) + GDN_HEADS, 1.0, 0.0).astype(BF)
    c1, c2, c3 = _split3(gc)
    gcb = (jnp.dot(c1, e_g, preferred_element_type=F32)
           + jnp.dot(c2, e_g, preferred_element_type=F32)
           + jnp.dot(c3, e_g, preferred_element_type=F32))
    b1, b2, _ = _split3(beta)
    betab = (jnp.dot(b1, e_b, preferred_element_type=F32)
             + jnp.dot(b2, e_b, preferred_element_type=F32))
    gct = gc.T

    outs = []
    for h in range(GDN_HEADS):
        hs = slice(GDN_DIM * h, GDN_DIM * (h + 1))
        qh = qc[:, hs]
        kh = kc[:, hs]
        vh = vc[:, hs]
        qh = qh * lax.rsqrt(jnp.sum(qh * qh, axis=-1, keepdims=True) + EPS) * (GDN_DIM ** -0.5)
        kh = kh * lax.rsqrt(jnp.sum(kh * kh, axis=-1, keepdims=True) + EPS)
        bh = betab[:, hs]
        gh = gcb[:, hs]
        egh = jnp.exp(gh)
        kbh = kh * bh
        vbh = vh * bh
        st = state_ref[h]
        o_chunks = []
        for c in range(GDN_NC):
            rs = slice(CHUNK * c, CHUNK * (c + 1))
            k_ = kh[rs].astype(BF)
            q_ = qh[rs].astype(BF)
            kb_ = kbh[rs]
            col = gh[rs, :CHUNK]
            row = gct[h:h + 1, CHUNK * c: CHUNK * (c + 1)]
            ii = lax.broadcasted_iota(jnp.int32, (CHUNK, CHUNK), 0)
            jj = lax.broadcasted_iota(jnp.int32, (CHUNK, CHUNK), 1)
            low = ii >= jj
            dec = jnp.where(low, jnp.exp(jnp.where(low, col - row, 0.0)), 0.0)
            kk = _dot_nt(kb_.astype(BF), k_)
            qk = _dot_nt(q_, k_)
            m = jnp.where(ii > jj, -(kk * dec), 0.0)
            attn = qk * dec
            eye = jnp.where(ii == jj, 1.0, 0.0)
            pinv = eye + m
            mp = m
            for _ in range(5):
                mpb = mp.astype(BF)
                mp = jnp.dot(mpb, mpb, preferred_element_type=F32)
                pinv = pinv + jnp.dot(pinv.astype(BF), mp.astype(BF), preferred_element_type=F32)
            rhs = jnp.concatenate([vbh[rs], kb_ * egh[rs]], axis=1).astype(BF)
            sol = jnp.dot(pinv.astype(BF), rhs, preferred_element_type=F32)
            u = sol[:, :GDN_DIM]
            w = sol[:, GDN_DIM:]
            gl = gh[CHUNK * (c + 1) - 1: CHUNK * (c + 1), :]
            qd = (qh[rs] * egh[rs]).astype(BF)
            kd = (kh[rs] * jnp.exp(gl - gh[rs])).astype(BF)
            stb = st.astype(BF)
            v_new = u - jnp.dot(w.astype(BF), stb, preferred_element_type=F32)
            vnb = v_new.astype(BF)
            o_c = (jnp.dot(qd, stb, preferred_element_type=F32)
                   + jnp.dot(attn.astype(BF), vnb, preferred_element_type=F32))
            st = st * jnp.exp(gl) + _dot_tn(kd, vnb)
            o_chunks.append(o_c)
        state_ref[h] = st
        o_h = jnp.concatenate(o_chunks, axis=0)
        o_h = _rms(o_h, og_ref[...])
        gt = gate_ref[:, hs].astype(F32)
        o_ref[:, hs] = (o_h * (gt * jax.nn.sigmoid(gt))).astype(BF)


def _gdn(p, g, cw, arow, dtrow, og, batch, seq):
    n = p.shape[0]
    T = GDN_T
    nt = seq // T
    tok = lambda col: pl.BlockSpec((T, GDN_W), lambda b, t: (b * nt + t, col // GDN_W))
    halo = lambda col: pl.BlockSpec(
        (8, GDN_W), lambda b, t: (jnp.maximum((b * nt + t) * (T // 8) - 1, 0), col // GDN_W))
    full = lambda r, c: pl.BlockSpec((r, c), lambda b, t: (0, 0))
    return pl.pallas_call(
        _gdn_kernel,
        grid=(batch, nt),
        in_specs=[
            tok(P_GQ), tok(P_GK), tok(P_GV), halo(P_GQ), halo(P_GK), halo(P_GV),
            pl.BlockSpec((T, 128), lambda b, t: (b * nt + t, 0)),
            tok(P_GG),
            full(CONV_W, 3 * GDN_W), full(1, 128), full(1, 128), full(1, GDN_DIM),
        ],
        out_specs=pl.BlockSpec((T, GDN_W), lambda b, t: (b * nt + t, 0)),
        out_shape=jax.ShapeDtypeStruct((n, GDN_W), BF),
        scratch_shapes=[
            pltpu.VMEM((GDN_HEADS, GDN_DIM, GDN_DIM), F32),
            pltpu.VMEM((T + 8, GDN_W), F32),
            pltpu.VMEM((T + 8, GDN_W), F32),
            pltpu.VMEM((T + 8, GDN_W), F32),
        ],
        compiler_params=_cparams(("parallel", "arbitrary")),
        name="gdn",
    )(p, p, p, p, p, p, g, p, cw, arow, dtrow, og)


OUT_TM, OUT_TN = 1024, 1024


def _outproj_kernel(a_ref, b_ref, wa_ref, wb_ref, x_ref, o_ref):
    acc = jnp.dot(a_ref[...], wa_ref[...], preferred_element_type=F32)
    acc = acc + jnp.dot(b_ref[...], wb_ref[...], preferred_element_type=F32)
    o_ref[...] = x_ref[...] + acc


def _out_proj(a, b, wa, wb, x2):
    n = x2.shape[0]
    half = a.shape[1]
    return pl.pallas_call(
        _outproj_kernel,
        grid=(n // OUT_TM, D_MODEL // OUT_TN),
        in_specs=[
            pl.BlockSpec((OUT_TM, half), lambda i, j: (i, 0)),
            pl.BlockSpec((OUT_TM, half), lambda i, j: (i, 0)),
            pl.BlockSpec((half, OUT_TN), lambda i, j: (0, j)),
            pl.BlockSpec((half, OUT_TN), lambda i, j: (0, j)),
            pl.BlockSpec((OUT_TM, OUT_TN), lambda i, j: (i, j)),
        ],
        out_specs=pl.BlockSpec((OUT_TM, OUT_TN), lambda i, j: (i, j)),
        out_shape=jax.ShapeDtypeStruct((n, D_MODEL), F32),
        compiler_params=_cparams(("parallel", "arbitrary")),
        name="out_proj",
    )(a, b, wa, wb, x2)


def _rot_cols(w):
    return jnp.concatenate([-w[..., ROPE // 2:], w[..., :ROPE // 2]], axis=-1)


def _swap_halves(g):
    return jnp.concatenate([g[..., ROPE // 2:], g[..., :ROPE // 2]], axis=-1)


def _prep_w_in(w):
    cq, ckv, kr, mg, gq, gk, gv, ga, gb, gg = jnp.split(
        w, [512, 768, 832, 1856, 2880, 3904, 4928, 4936, 4944], axis=1)
    pad = jnp.zeros((w.shape[0], P_COLS - P_G - 16), w.dtype)
    return jnp.concatenate([gq, gk, gv, gg, mg, cq, ckv, kr, _rot_cols(kr), ga, gb, pad],
                           axis=1).astype(BF)


def _prep_w_uq(w):
    w3 = w.reshape(Q_LORA, MLA_HEADS, QK_DIM)
    rope = w3[..., NOPE:]
    return jnp.concatenate([w3, _rot_cols(rope)], axis=-1).reshape(Q_LORA, MLA_HEADS * HEAD_SLAB).astype(BF)


def _prep_qk_gain(g):
    rope = g[NOPE:]
    return jnp.concatenate([g, _swap_halves(rope)])[None, :]


def kernel(x, positions, norm_gain, w_in, mla_q_a_gain, mla_kv_a_gain, w_uq, w_ukv,
           mla_q_norm_gain, mla_k_norm_gain, gdn_conv_w, gdn_a_log, gdn_dt_bias,
           gdn_out_norm_gain, w_out):
    batch, seq, _ = x.shape
    n = batch * seq
    x2 = x.reshape(n, D_MODEL)
    pos = positions.reshape(n, 1)
    half = ROPE // 2
    inv_freq = jnp.power(ROPE_THETA, -jnp.arange(half, dtype=F32) / half)
    invf = jnp.tile(inv_freq, 4)[None, :]

    h = x2
    for layer in range(w_in.shape[0]):
        p, g = _in_proj(h, norm_gain[layer][None, :], _prep_w_in(w_in[layer]))
        qp, kp, vp = _mla_prep(
            p, pos, mla_q_a_gain[layer][None, :], mla_kv_a_gain[layer][None, :],
            _prep_w_uq(w_uq[layer]), w_ukv[layer].astype(BF),
            _prep_qk_gain(mla_q_norm_gain[layer]), _prep_qk_gain(mla_k_norm_gain[layer]), invf)
        o_mla = _mla_attn(qp, kp, vp, p, batch, seq)
        zpad = jnp.zeros((128 - GDN_HEADS,), F32)
        arow = jnp.concatenate([gdn_a_log[layer].astype(F32), zpad])[None, :]
        dtrow = jnp.concatenate([gdn_dt_bias[layer].astype(F32), zpad])[None, :]
        o_gdn = _gdn(p, g, gdn_conv_w[layer], arow, dtrow, gdn_out_norm_gain[layer][None, :], batch, seq)
        wo = w_out[layer].astype(BF)
        h = _out_proj(o_mla, o_gdn, wo[:MLA_HEADS * V_DIM], wo[MLA_HEADS * V_DIM:], h)
    return h.reshape(batch, seq, D_MODEL)
```

```python
import functools
import math

import jax
import jax.numpy as jnp
from jax import lax
from jax.experimental import pallas as pl
from jax.experimental.pallas import tpu as pltpu

BF = jnp.bfloat16
F32 = jnp.float32

D_MODEL = 2048
MLA_HEADS = 8
NOPE = 128
ROPE = 64
QK_DIM = NOPE + ROPE
V_DIM = 128
Q_LORA = 512
KV_LORA = 256
GDN_HEADS = 8
GDN_DIM = 128
GDN_W = GDN_HEADS * GDN_DIM
CONV_W = 4
CHUNK = 64
ROPE_THETA = 10000.0
EPS = 1e-6
LOG2E = math.log2(math.e)

P_GQ, P_GK, P_GV, P_GG, P_MG = 0, 1024, 2048, 3072, 4096
P_CQ, P_CKV, P_KR, P_G = 5120, 5632, 5888, 6016
P_COLS = 6144

VMEM_LIMIT = 56 * 1024 * 1024


def _cparams(sem):
    return pltpu.CompilerParams(dimension_semantics=sem, vmem_limit_bytes=VMEM_LIMIT)


IN_TM, IN_TN = 1024, 768


def _inproj_kernel(x_ref, gain_ref, w_ref, p_ref, g_ref, xn_ref):
    j = pl.program_id(1)

    @pl.when(j == 0)
    def _():
        xf = x_ref[...]
        ms = jnp.mean(xf * xf, axis=-1, keepdims=True)
        xn_ref[...] = (xf * lax.rsqrt(ms + EPS) * gain_ref[...]).astype(BF)

    acc = jnp.dot(xn_ref[...], w_ref[...], preferred_element_type=F32)
    p_ref[...] = acc.astype(BF)

    @pl.when(j == pl.num_programs(1) - 1)
    def _():
        g_ref[...] = acc[:, IN_TN - 128:]


def _in_proj(x2, gain, w):
    n = x2.shape[0]
    return pl.pallas_call(
        _inproj_kernel,
        grid=(n // IN_TM, P_COLS // IN_TN),
        in_specs=[
            pl.BlockSpec((IN_TM, D_MODEL), lambda i, j: (i, 0)),
            pl.BlockSpec((1, D_MODEL), lambda i, j: (0, 0)),
            pl.BlockSpec((D_MODEL, IN_TN), lambda i, j: (0, j)),
        ],
        out_specs=[
            pl.BlockSpec((IN_TM, IN_TN), lambda i, j: (i, j)),
            pl.BlockSpec((IN_TM, 128), lambda i, j: (i, 0)),
        ],
        out_shape=[
            jax.ShapeDtypeStruct((n, P_COLS), BF),
            jax.ShapeDtypeStruct((n, 128), F32),
        ],
        scratch_shapes=[pltpu.VMEM((IN_TM, D_MODEL), BF)],
        compiler_params=_cparams(("parallel", "arbitrary")),
        name="in_proj",
    )(x2, gain, w)


PREP_TM = 512
HEAD_SLAB = 256


def _rms(x, gain):
    ms = jnp.mean(x * x, axis=-1, keepdims=True)
    return x * lax.rsqrt(ms + EPS) * gain


def _mla_prep_kernel(cq_ref, ckv_ref, kr_ref, pos_ref, qag_ref, kvag_ref, wuq_ref, wukv_ref,
                     qg_ref, kg_ref, invf_ref, q_ref, k_ref, v_ref):
    lane = lax.broadcasted_iota(jnp.int32, (1, 128), 1)
    lo64 = lane < ROPE

    ang = pos_ref[...].astype(F32) * invf_ref[...]
    cs = jnp.where(lo64, jnp.cos(ang), jnp.sin(ang))

    cqn = _rms(cq_ref[...].astype(F32), qag_ref[...]).astype(BF)
    y = jnp.dot(cqn, wuq_ref[...], preferred_element_type=F32)
    qg0 = qg_ref[:, :128]
    qg1 = qg_ref[:, 128:]
    qscale = QK_DIM ** -0.5 * LOG2E
    for h in range(MLA_HEADS):
        y0 = y[:, HEAD_SLAB * h: HEAD_SLAB * h + 128]
        y1 = y[:, HEAD_SLAB * h + 128: HEAD_SLAB * (h + 1)]
        ss = (jnp.sum(y0 * y0, axis=-1, keepdims=True)
              + jnp.sum(jnp.where(lo64, y1 * y1, 0.0), axis=-1, keepdims=True))
        r = lax.rsqrt(ss * (1.0 / QK_DIM) + EPS)
        qn = y0 * r * qg0
        z = y1 * r * qg1 * cs
        qr = z + pltpu.roll(z, ROPE, axis=1)
        q_ref[:, HEAD_SLAB * h: HEAD_SLAB * h + 128] = (qn * qscale).astype(BF)
        q_ref[:, HEAD_SLAB * h + 128: HEAD_SLAB * (h + 1)] = (qr * qscale).astype(BF)

    ckvn = _rms(ckv_ref[...].astype(F32), kvag_ref[...]).astype(BF)
    kv = jnp.dot(ckvn, wukv_ref[...], preferred_element_type=F32)
    kr = kr_ref[...].astype(F32)
    ssr = jnp.sum(jnp.where(lo64, kr * kr, 0.0), axis=-1, keepdims=True)
    kg0 = kg_ref[:, :128]
    kg1 = kg_ref[:, 128:]
    zk0 = kr * kg1 * cs
    for h in range(MLA_HEADS):
        kn = kv[:, HEAD_SLAB * h: HEAD_SLAB * h + 128]
        ss = jnp.sum(kn * kn, axis=-1, keepdims=True) + ssr
        r = lax.rsqrt(ss * (1.0 / QK_DIM) + EPS)
        zk = zk0 * r
        krh = jnp.where(lo64, zk + pltpu.roll(zk, ROPE, axis=1), 0.0)
        k_ref[:, HEAD_SLAB * h: HEAD_SLAB * h + 128] = (kn * r * kg0).astype(BF)
        k_ref[:, HEAD_SLAB * h + 128: HEAD_SLAB * (h + 1)] = krh.astype(BF)
        v_ref[:, V_DIM * h: V_DIM * (h + 1)] = kv[:, HEAD_SLAB * h + 128: HEAD_SLAB * (h + 1)].astype(BF)


def _mla_prep(p, pos, qag, kvag, wuq, wukv, qg, kg, invf):
    n = p.shape[0]
    tm = PREP_TM
    full = lambda r, c: pl.BlockSpec((r, c), lambda i: (0, 0))
    return pl.pallas_call(
        _mla_prep_kernel,
        grid=(n // tm,),
        in_specs=[
            pl.BlockSpec((tm, Q_LORA), lambda i: (i, P_CQ // Q_LORA)),
            pl.BlockSpec((tm, KV_LORA), lambda i: (i, P_CKV // KV_LORA)),
            pl.BlockSpec((tm, 128), lambda i: (i, P_KR // 128)),
            pl.BlockSpec((tm, 1), lambda i: (i, 0)),
            full(1, Q_LORA), full(1, KV_LORA),
            full(Q_LORA, MLA_HEADS * HEAD_SLAB), full(KV_LORA, MLA_HEADS * HEAD_SLAB),
            full(1, HEAD_SLAB), full(1, HEAD_SLAB), full(1, 128),
        ],
        out_specs=[
            pl.BlockSpec((tm, MLA_HEADS * HEAD_SLAB), lambda i: (i, 0)),
            pl.BlockSpec((tm, MLA_HEADS * HEAD_SLAB), lambda i: (i, 0)),
            pl.BlockSpec((tm, MLA_HEADS * V_DIM), lambda i: (i, 0)),
        ],
        out_shape=[
            jax.ShapeDtypeStruct((n, MLA_HEADS * HEAD_SLAB), BF),
            jax.ShapeDtypeStruct((n, MLA_HEADS * HEAD_SLAB), BF),
            jax.ShapeDtypeStruct((n, MLA_HEADS * V_DIM), BF),
        ],
        compiler_params=_cparams(("parallel",)),
        name="mla_prep",
    )(p, p, p, pos, qag, kvag, wuq, wukv, qg, kg, invf)


ATT_TQ = 512
ATT_TK = 512
NEG = -1e30


def _attn_kernel(q_ref, k_ref, v_ref, gate_ref, o_ref, m_ref, l_ref, acc_ref):
    qi = pl.program_id(2)
    q = q_ref[...]
    m_ref[...] = jnp.full(m_ref.shape, NEG, F32)
    l_ref[...] = jnp.zeros(l_ref.shape, F32)
    acc_ref[...] = jnp.zeros(acc_ref.shape, F32)

    def step(j, masked):
        off = pl.multiple_of(j * ATT_TK, ATT_TK)
        kj = k_ref[pl.ds(off, ATT_TK), :]
        vj = v_ref[pl.ds(off, ATT_TK), :]
        s = lax.dot_general(q, kj, (((1,), (1,)), ((), ())), preferred_element_type=F32)
        if masked:
            row = lax.broadcasted_iota(jnp.int32, s.shape, 0)
            col = lax.broadcasted_iota(jnp.int32, s.shape, 1)
            s = jnp.where(col <= row, s, NEG)
        m_prev = m_ref[...]
        m_new = jnp.maximum(m_prev, jnp.max(s, axis=-1, keepdims=True))
        alpha = jnp.exp2(m_prev - m_new)
        p = jnp.exp2(s - pltpu.repeat(m_new, ATT_TK // 128, axis=1))
        psum = p[:, :128]
        for c in range(1, ATT_TK // 128):
            psum = psum + p[:, 128 * c: 128 * (c + 1)]
        l_ref[...] = alpha * l_ref[...] + psum
        acc_ref[...] = alpha * acc_ref[...] + jnp.dot(p.astype(BF), vj, preferred_element_type=F32)
        m_ref[...] = m_new

    def body(j, carry):
        step(j, False)
        return carry

    lax.fori_loop(0, qi, body, 0)
    step(qi, True)

    g = gate_ref[...].astype(F32)
    o = acc_ref[...] * (1.0 / jnp.sum(l_ref[...], axis=-1, keepdims=True))
    o_ref[...] = (o * (g * jax.nn.sigmoid(g))).astype(BF)


def _mla_attn(qp, kp, vp, p, batch, seq):
    n = qp.shape[0]
    nq = seq // ATT_TQ
    return pl.pallas_call(
        _attn_kernel,
        grid=(batch, MLA_HEADS, nq),
        in_specs=[
            pl.BlockSpec((ATT_TQ, HEAD_SLAB), lambda b, h, i: (b * nq + i, h)),
            pl.BlockSpec((seq, HEAD_SLAB), lambda b, h, i: (b, h)),
            pl.BlockSpec((seq, V_DIM), lambda b, h, i: (b, h)),
            pl.BlockSpec((ATT_TQ, V_DIM), lambda b, h, i: (b * nq + i, P_MG // V_DIM + h)),
        ],
        out_specs=pl.BlockSpec((ATT_TQ, V_DIM), lambda b, h, i: (b * nq + i, h)),
        out_shape=jax.ShapeDtypeStruct((n, MLA_HEADS * V_DIM), BF),
        scratch_shapes=[
            pltpu.VMEM((ATT_TQ, 128), F32),
            pltpu.VMEM((ATT_TQ, 128), F32),
            pltpu.VMEM((ATT_TQ, V_DIM), F32),
        ],
        compiler_params=_cparams(("parallel", "parallel", "arbitrary")),
        name="mla_attn",
    )(qp, kp, vp, p)


GDN_T = 128
GDN_NC = GDN_T // CHUNK


def _split3(x):
    a = x.astype(BF)
    r1 = x - a.astype(F32)
    b = r1.astype(BF)
    c = (r1 - b.astype(F32)).astype(BF)
    return a, b, c


def _dot_nt(a, b):
    return lax.dot_general(a, b, (((1,), (1,)), ((), ())), preferred_element_type=F32)


def _dot_tn(a, b):
    return lax.dot_general(a, b, (((0,), (0,)), ((), ())), preferred_element_type=F32)


def _gdn_kernel(q_ref, k_ref, v_ref, hq_ref, hk_ref, hv_ref, g_ref, gate_ref,
                cw_ref, arow_ref, dtrow_ref, og_ref, o_ref,
                state_ref, xq_ref, xk_ref, xv_ref):
    t = pl.program_id(1)
    T = GDN_T

    @pl.when(t == 0)
    def _():
        state_ref[...] = jnp.zeros(state_ref.shape, F32)

    first = t == 0

    def conv_silu(x_ref, h_ref, xp_ref, col0):
        halo = jnp.where(first, 0.0, h_ref[...].astype(F32))
        xp_ref[0:8, :] = halo
        xp_ref[8:8 + T, :] = x_ref[...].astype(F32)
        y = xp_ref[pl.ds(5, T), :] * cw_ref[0:1, col0:col0 + GDN_W]
        for j in range(1, CONV_W):
            y = y + xp_ref[pl.ds(5 + j, T), :] * cw_ref[j:j + 1, col0:col0 + GDN_W]
        return y * jax.nn.sigmoid(y)

    qc = conv_silu(q_ref, hq_ref, xq_ref, 0)
    kc = conv_silu(k_ref, hk_ref, xk_ref, GDN_W)
    vc = conv_silu(v_ref, hv_ref, xv_ref, 2 * GDN_W)

    gin = g_ref[...]
    gx = gin + dtrow_ref[...]
    softplus = jnp.maximum(gx, 0.0) + jnp.log1p(jnp.exp(-jnp.abs(gx)))
    gdec = -jnp.exp(arow_ref[...]) * softplus
    beta = jax.nn.sigmoid(gin)

    ri = lax.broadcasted_iota(jnp.int32, (T, T), 0)
    ci = lax.broadcasted_iota(jnp.int32, (T, T), 1)
    same_chunk = jnp.right_shift(ri, 6) == jnp.right_shift(ci, 6)
    tri = jnp.where(same_chunk, jnp.where(ci <= ri, 1.0, 0.0), 0.0).astype(BF)
    g1, g2, g3 = _split3(gdec)
    gc = (jnp.dot(tri, g1, preferred_element_type=F32)
          + jnp.dot(tri, g2, preferred_element_type=F32)
          + jnp.dot(tri, g3, preferred_element_type=F32))

    er = lax.broadcasted_iota(jnp.int32, (128, GDN_W), 0)
    ec = lax.broadcasted_iota(jnp.int32, (128, GDN_W), 1)
    e_g = jnp.where(er == jnp.right_shift(ec, 7), 1.0, 0.0).astype(BF)
    e_b = jnp.where(er == jnp.right_shift(ec, 7) + GDN_HEADS, 1.0, 0.0).astype(BF)
    c1, c2, c3 = _split3(gc)
    gcb = (jnp.dot(c1, e_g, preferred_element_type=F32)
           + jnp.dot(c2, e_g, preferred_element_type=F32)
           + jnp.dot(c3, e_g, preferred_element_type=F32))
    b1, b2, _ = _split3(beta)
    betab = (jnp.dot(b1, e_b, preferred_element_type=F32)
             + jnp.dot(b2, e_b, preferred_element_type=F32))
    gct = gc.T

    lane = lax.broadcasted_iota(jnp.int32, (1, 128), 1)
    lm0 = lane < CHUNK
    ii = lax.broadcasted_iota(jnp.int32, (CHUNK, 128), 0)
    jm = jnp.bitwise_and(lax.broadcasted_iota(jnp.int32, (CHUNK, 128), 1), CHUNK - 1)
    low = ii >= jm
    strict = ii > jm
    eye_p = jnp.where(ii == jm, 1.0, 0.0)
    heads = range(GDN_HEADS)

    def bd(xp):
        return jnp.concatenate([jnp.where(lm0, xp, 0.0), jnp.where(lm0, 0.0, xp)], axis=0).astype(BF)

    def mm(a, b):
        return jnp.dot(a, b, preferred_element_type=F32)

    qd, rhs, kdt, egl, kq, dec = [], [], [], [], [], []
    for h in heads:
        hs = slice(GDN_DIM * h, GDN_DIM * (h + 1))
        qh = qc[:, hs]
        kh = kc[:, hs]
        qh = qh * (lax.rsqrt(jnp.sum(qh * qh, axis=-1, keepdims=True) + EPS) * (GDN_DIM ** -0.5))
        kh = kh * lax.rsqrt(jnp.sum(kh * kh, axis=-1, keepdims=True) + EPS)
        bh = betab[:, hs]
        gh = gcb[:, hs]
        egh = jnp.exp(gh)
        kbh = kh * bh
        kt = kh.T
        rowp = gct[h:h + 1, :]
        lhs = jnp.concatenate(
            [jnp.concatenate([kbh[:CHUNK], kbh[CHUNK:]], axis=1),
             jnp.concatenate([qh[:CHUNK], qh[CHUNK:]], axis=1)], axis=0).astype(BF)
        bdt = jnp.concatenate([jnp.where(lm0, kt, 0.0), jnp.where(lm0, 0.0, kt)], axis=0).astype(BF)
        kq.append(mm(lhs, bdt))
        colp = jnp.where(lm0, gh[:CHUNK], gh[CHUNK:])
        dec.append(jnp.where(low, jnp.exp(jnp.where(low, colp - rowp, 0.0)), 0.0))
        rhs.append(jnp.concatenate([vc[:, hs] * bh, kbh * egh], axis=1).astype(BF))
        qd.append((qh * egh).astype(BF))
        gl0 = gh[CHUNK - 1:CHUNK, :]
        gl1 = gh[2 * CHUNK - 1:2 * CHUNK, :]
        rf0 = jnp.where(lm0, jnp.exp(jnp.where(lm0, gl0 - rowp, 0.0)), 0.0)
        rf1 = jnp.where(lm0, 0.0, jnp.exp(jnp.where(lm0, 0.0, gl1 - rowp)))
        kdt.append(((kt * rf0).astype(BF), (kt * rf1).astype(BF)))
        egl.append((jnp.exp(gl0), jnp.exp(gl1)))

    mneg = [jnp.where(strict, -(kq[h][:CHUNK] * dec[h]), 0.0) for h in heads]
    attn = [kq[h][CHUNK:] * dec[h] for h in heads]
    pinv = [eye_p + mneg[h] for h in heads]
    mp = [mm(mneg[h].astype(BF), bd(mneg[h])) for h in heads]
    for _ in range(4):
        r = [mm(jnp.concatenate([pinv[h], mp[h]], axis=0).astype(BF), bd(mp[h])) for h in heads]
        pinv = [pinv[h] + r[h][:CHUNK] for h in heads]
        mp = [r[h][CHUNK:] for h in heads]
    pinv = [pinv[h] + mm(pinv[h].astype(BF), bd(mp[h])) for h in heads]
    sol = [mm(bd(pinv[h]), rhs[h]) for h in heads]

    st = [state_ref[h] for h in heads]
    o_parts = [[] for _ in heads]
    zeros_c = jnp.zeros((CHUNK, GDN_DIM), BF)
    for c in range(GDN_NC):
        rs = slice(CHUNK * c, CHUNK * (c + 1))
        r1 = [mm(jnp.concatenate([sol[h][rs, GDN_DIM:].astype(BF), qd[h][rs]], axis=0), st[h].astype(BF))
              for h in heads]
        for h in heads:
            vn = (sol[h][rs, :GDN_DIM] - r1[h][:CHUNK]).astype(BF)
            vfull = jnp.concatenate([vn, zeros_c] if c == 0 else [zeros_c, vn], axis=0)
            am = jnp.where(lm0, attn[h], 0.0) if c == 0 else jnp.where(lm0, 0.0, attn[h])
            r2 = mm(jnp.concatenate([am.astype(BF), kdt[h][c]], axis=0), vfull)
            o_parts[h].append(r1[h][CHUNK:] + r2[:CHUNK])
            st[h] = st[h] * egl[h][c] + r2[CHUNK:]
    for h in heads:
        hs = slice(GDN_DIM * h, GDN_DIM * (h + 1))
        state_ref[h] = st[h]
        o_h = _rms(jnp.concatenate(o_parts[h], axis=0), og_ref[...])
        gt = gate_ref[:, hs].astype(F32)
        o_ref[:, hs] = (o_h * (gt * jax.nn.sigmoid(gt))).astype(BF)


def _gdn(p, g, cw, arow, dtrow, og, batch, seq):
    n = p.shape[0]
    T = GDN_T
    nt = seq // T
    tok = lambda col: pl.BlockSpec((T, GDN_W), lambda b, t: (b * nt + t, col // GDN_W))
    halo = lambda col: pl.BlockSpec(
        (8, GDN_W), lambda b, t: (jnp.maximum((b * nt + t) * (T // 8) - 1, 0), col // GDN_W))
    full = lambda r, c: pl.BlockSpec((r, c), lambda b, t: (0, 0))
    return pl.pallas_call(
        _gdn_kernel,
        grid=(batch, nt),
        in_specs=[
            tok(P_GQ), tok(P_GK), tok(P_GV), halo(P_GQ), halo(P_GK), halo(P_GV),
            pl.BlockSpec((T, 128), lambda b, t: (b * nt + t, 0)),
            tok(P_GG),
            full(CONV_W, 3 * GDN_W), full(1, 128), full(1, 128), full(1, GDN_DIM),
        ],
        out_specs=pl.BlockSpec((T, GDN_W), lambda b, t: (b * nt + t, 0)),
        out_shape=jax.ShapeDtypeStruct((n, GDN_W), BF),
        scratch_shapes=[
            pltpu.VMEM((GDN_HEADS, GDN_DIM, GDN_DIM), F32),
            pltpu.VMEM((T + 8, GDN_W), F32),
            pltpu.VMEM((T + 8, GDN_W), F32),
            pltpu.VMEM((T + 8, GDN_W), F32),
        ],
        compiler_params=_cparams(("parallel", "arbitrary")),
        name="gdn",
    )(p, p, p, p, p, p, g, p, cw, arow, dtrow, og)


OUT_TM, OUT_TN = 1024, 1024


def _outproj_kernel(a_ref, b_ref, wa_ref, wb_ref, x_ref, o_ref):
    acc = jnp.dot(a_ref[...], wa_ref[...], preferred_element_type=F32)
    acc = acc + jnp.dot(b_ref[...], wb_ref[...], preferred_element_type=F32)
    o_ref[...] = x_ref[...] + acc


def _out_proj(a, b, wa, wb, x2):
    n = x2.shape[0]
    half = a.shape[1]
    return pl.pallas_call(
        _outproj_kernel,
        grid=(n // OUT_TM, D_MODEL // OUT_TN),
        in_specs=[
            pl.BlockSpec((OUT_TM, half), lambda i, j: (i, 0)),
            pl.BlockSpec((OUT_TM, half), lambda i, j: (i, 0)),
            pl.BlockSpec((half, OUT_TN), lambda i, j: (0, j)),
            pl.BlockSpec((half, OUT_TN), lambda i, j: (0, j)),
            pl.BlockSpec((OUT_TM, OUT_TN), lambda i, j: (i, j)),
        ],
        out_specs=pl.BlockSpec((OUT_TM, OUT_TN), lambda i, j: (i, j)),
        out_shape=jax.ShapeDtypeStruct((n, D_MODEL), F32),
        compiler_params=_cparams(("parallel", "arbitrary")),
        name="out_proj",
    )(a, b, wa, wb, x2)


def _rot_cols(w):
    return jnp.concatenate([-w[..., ROPE // 2:], w[..., :ROPE // 2]], axis=-1)


def _swap_halves(g):
    return jnp.concatenate([g[..., ROPE // 2:], g[..., :ROPE // 2]], axis=-1)


def _prep_w_in(w):
    cq, ckv, kr, mg, gq, gk, gv, ga, gb, gg = jnp.split(
        w, [512, 768, 832, 1856, 2880, 3904, 4928, 4936, 4944], axis=1)
    pad = jnp.zeros((w.shape[0], P_COLS - P_G - 16), w.dtype)
    return jnp.concatenate([gq, gk, gv, gg, mg, cq, ckv, kr, _rot_cols(kr), ga, gb, pad],
                           axis=1).astype(BF)


def _prep_w_uq(w):
    w3 = w.reshape(Q_LORA, MLA_HEADS, QK_DIM)
    rope = w3[..., NOPE:]
    return jnp.concatenate([w3, _rot_cols(rope)], axis=-1).reshape(Q_LORA, MLA_HEADS * HEAD_SLAB).astype(BF)


def _prep_qk_gain(g):
    rope = g[NOPE:]
    return jnp.concatenate([g, _swap_halves(rope)])[None, :]


def kernel(x, positions, norm_gain, w_in, mla_q_a_gain, mla_kv_a_gain, w_uq, w_ukv,
           mla_q_norm_gain, mla_k_norm_gain, gdn_conv_w, gdn_a_log, gdn_dt_bias,
           gdn_out_norm_gain, w_out):
    batch, seq, _ = x.shape
    n = batch * seq
    x2 = x.reshape(n, D_MODEL)
    pos = positions.reshape(n, 1)
    half = ROPE // 2
    inv_freq = jnp.power(ROPE_THETA, -jnp.arange(half, dtype=F32) / half)
    invf = jnp.tile(inv_freq, 4)[None, :]

    h = x2
    for layer in range(w_in.shape[0]):
        p, g = _in_proj(h, norm_gain[layer][None, :], _prep_w_in(w_in[layer]))
        qp, kp, vp = _mla_prep(
            p, pos, mla_q_a_gain[layer][None, :], mla_kv_a_gain[layer][None, :],
            _prep_w_uq(w_uq[layer]), w_ukv[layer].astype(BF),
            _prep_qk_gain(mla_q_norm_gain[layer]), _prep_qk_gain(mla_k_norm_gain[layer]), invf)
        o_mla = _mla_attn(qp, kp, vp, p, batch, seq)
        zpad = jnp.zeros((128 - GDN_HEADS,), F32)
        arow = jnp.concatenate([gdn_a_log[layer].astype(F32), zpad])[None, :]
        dtrow = jnp.concatenate([gdn_dt_bias[layer].astype(F32), zpad])[None, :]
        o_gdn = _gdn(p, g, gdn_conv_w[layer], arow, dtrow, gdn_out_norm_gain[layer][None, :], batch, seq)
        wo = w_out[layer].astype(BF)
        h = _out_proj(o_mla, o_gdn, wo[:MLA_HEADS * V_DIM], wo[MLA_HEADS * V_DIM:], h)
    return h.reshape(batch, seq, D_MODEL)
```

```python
import functools
import math

import jax
import jax.numpy as jnp
from jax import lax
from jax.experimental import pallas as pl
from jax.experimental.pallas import tpu as pltpu

BF = jnp.bfloat16
F32 = jnp.float32

D_MODEL = 2048
MLA_HEADS = 8
NOPE = 128
ROPE = 64
QK_DIM = NOPE + ROPE
V_DIM = 128
Q_LORA = 512
KV_LORA = 256
GDN_HEADS = 8
GDN_DIM = 128
GDN_W = GDN_HEADS * GDN_DIM
CONV_W = 4
CHUNK = 64
ROPE_THETA = 10000.0
EPS = 1e-6
LOG2E = math.log2(math.e)

P_GQ, P_GK, P_GV, P_GG, P_MG = 0, 1024, 2048, 3072, 4096
P_CQ, P_CKV, P_KR, P_G = 5120, 5632, 5888, 6016
P_COLS = 6144

VMEM_LIMIT = 56 * 1024 * 1024


def _cparams(sem):
    return pltpu.CompilerParams(dimension_semantics=sem, vmem_limit_bytes=VMEM_LIMIT)


IN_TM, IN_TN = 1024, 768


def _inproj_kernel(x_ref, gain_ref, w_ref, p_ref, g_ref, xn_ref):
    j = pl.program_id(1)

    @pl.when(j == 0)
    def _():
        xf = x_ref[...]
        ms = jnp.mean(xf * xf, axis=-1, keepdims=True)
        xn_ref[...] = (xf * lax.rsqrt(ms + EPS) * gain_ref[...]).astype(BF)

    acc = jnp.dot(xn_ref[...], w_ref[...], preferred_element_type=F32)
    p_ref[...] = acc.astype(BF)

    @pl.when(j == pl.num_programs(1) - 1)
    def _():
        g_ref[...] = acc[:, IN_TN - 128:]


def _in_proj(x2, gain, w):
    n = x2.shape[0]
    return pl.pallas_call(
        _inproj_kernel,
        grid=(n // IN_TM, P_COLS // IN_TN),
        in_specs=[
            pl.BlockSpec((IN_TM, D_MODEL), lambda i, j: (i, 0)),
            pl.BlockSpec((1, D_MODEL), lambda i, j: (0, 0)),
            pl.BlockSpec((D_MODEL, IN_TN), lambda i, j: (0, j)),
        ],
        out_specs=[
            pl.BlockSpec((IN_TM, IN_TN), lambda i, j: (i, j)),
            pl.BlockSpec((IN_TM, 128), lambda i, j: (i, 0)),
        ],
        out_shape=[
            jax.ShapeDtypeStruct((n, P_COLS), BF),
            jax.ShapeDtypeStruct((n, 128), F32),
        ],
        scratch_shapes=[pltpu.VMEM((IN_TM, D_MODEL), BF)],
        compiler_params=_cparams(("parallel", "arbitrary")),
        name="in_proj",
    )(x2, gain, w)


PREP_TM = 512
HEAD_SLAB = 256


def _rms(x, gain):
    ms = jnp.mean(x * x, axis=-1, keepdims=True)
    return x * lax.rsqrt(ms + EPS) * gain


def _mla_prep_kernel(cq_ref, ckv_ref, kr_ref, pos_ref, qag_ref, kvag_ref, wuq_ref, wukv_ref,
                     qg_ref, kg_ref, invf_ref, q_ref, k_ref, v_ref):
    lane = lax.broadcasted_iota(jnp.int32, (1, 128), 1)
    lo64 = lane < ROPE

    ang = pos_ref[...].astype(F32) * invf_ref[...]
    cs = jnp.where(lo64, jnp.cos(ang), jnp.sin(ang))

    cqn = _rms(cq_ref[...].astype(F32), qag_ref[...]).astype(BF)
    y = jnp.dot(cqn, wuq_ref[...], preferred_element_type=F32)
    qg0 = qg_ref[:, :128]
    qg1 = qg_ref[:, 128:]
    qscale = QK_DIM ** -0.5 * LOG2E
    for h in range(MLA_HEADS):
        y0 = y[:, HEAD_SLAB * h: HEAD_SLAB * h + 128]
        y1 = y[:, HEAD_SLAB * h + 128: HEAD_SLAB * (h + 1)]
        ss = (jnp.sum(y0 * y0, axis=-1, keepdims=True)
              + jnp.sum(jnp.where(lo64, y1 * y1, 0.0), axis=-1, keepdims=True))
        r = lax.rsqrt(ss * (1.0 / QK_DIM) + EPS)
        qn = y0 * r * qg0
        z = y1 * r * qg1 * cs
        qr = z + pltpu.roll(z, ROPE, axis=1)
        q_ref[:, HEAD_SLAB * h: HEAD_SLAB * h + 128] = (qn * qscale).astype(BF)
        q_ref[:, HEAD_SLAB * h + 128: HEAD_SLAB * (h + 1)] = (qr * qscale).astype(BF)

    ckvn = _rms(ckv_ref[...].astype(F32), kvag_ref[...]).astype(BF)
    kv = jnp.dot(ckvn, wukv_ref[...], preferred_element_type=F32)
    kr = kr_ref[...].astype(F32)
    ssr = jnp.sum(jnp.where(lo64, kr * kr, 0.0), axis=-1, keepdims=True)
    kg0 = kg_ref[:, :128]
    kg1 = kg_ref[:, 128:]
    zk0 = kr * kg1 * cs
    for h in range(MLA_HEADS):
        kn = kv[:, HEAD_SLAB * h: HEAD_SLAB * h + 128]
        ss = jnp.sum(kn * kn, axis=-1, keepdims=True) + ssr
        r = lax.rsqrt(ss * (1.0 / QK_DIM) + EPS)
        zk = zk0 * r
        krh = jnp.where(lo64, zk + pltpu.roll(zk, ROPE, axis=1), 0.0)
        k_ref[:, HEAD_SLAB * h: HEAD_SLAB * h + 128] = (kn * r * kg0).astype(BF)
        k_ref[:, HEAD_SLAB * h + 128: HEAD_SLAB * (h + 1)] = krh.astype(BF)
        v_ref[:, V_DIM * h: V_DIM * (h + 1)] = kv[:, HEAD_SLAB * h + 128: HEAD_SLAB * (h + 1)].astype(BF)


def _mla_prep(p, pos, qag, kvag, wuq, wukv, qg, kg, invf):
    n = p.shape[0]
    tm = PREP_TM
    full = lambda r, c: pl.BlockSpec((r, c), lambda i: (0, 0))
    return pl.pallas_call(
        _mla_prep_kernel,
        grid=(n // tm,),
        in_specs=[
            pl.BlockSpec((tm, Q_LORA), lambda i: (i, P_CQ // Q_LORA)),
            pl.BlockSpec((tm, KV_LORA), lambda i: (i, P_CKV // KV_LORA)),
            pl.BlockSpec((tm, 128), lambda i: (i, P_KR // 128)),
            pl.BlockSpec((tm, 1), lambda i: (i, 0)),
            full(1, Q_LORA), full(1, KV_LORA),
            full(Q_LORA, MLA_HEADS * HEAD_SLAB), full(KV_LORA, MLA_HEADS * HEAD_SLAB),
            full(1, HEAD_SLAB), full(1, HEAD_SLAB), full(1, 128),
        ],
        out_specs=[
            pl.BlockSpec((tm, MLA_HEADS * HEAD_SLAB), lambda i: (i, 0)),
            pl.BlockSpec((tm, MLA_HEADS * HEAD_SLAB), lambda i: (i, 0)),
            pl.BlockSpec((tm, MLA_HEADS * V_DIM), lambda i: (i, 0)),
        ],
        out_shape=[
            jax.ShapeDtypeStruct((n, MLA_HEADS * HEAD_SLAB), BF),
            jax.ShapeDtypeStruct((n, MLA_HEADS * HEAD_SLAB), BF),
            jax.ShapeDtypeStruct((n, MLA_HEADS * V_DIM), BF),
        ],
        compiler_params=_cparams(("parallel",)),
        name="mla_prep",
    )(p, p, p, pos, qag, kvag, wuq, wukv, qg, kg, invf)


ATT_TQ = 512
ATT_TK = 512
ATT_HG = 2
ATT_NS = 2 * ATT_HG
NEG = -1e30


def _attn_kernel(q_ref, k_ref, v_ref, gate_ref, o_ref, m_ref, l_ref, acc_ref):
    qi = pl.program_id(2)
    m_ref[...] = jnp.full(m_ref.shape, NEG, F32)
    l_ref[...] = jnp.zeros(l_ref.shape, F32)
    acc_ref[...] = jnp.zeros(acc_ref.shape, F32)
    nrep = ATT_TK // 128

    def scores(hh, j):
        off = pl.multiple_of(j * ATT_TK, ATT_TK)
        q = q_ref[:, HEAD_SLAB * hh: HEAD_SLAB * (hh + 1)]
        kj = k_ref[pl.ds(off, ATT_TK), HEAD_SLAB * hh: HEAD_SLAB * (hh + 1)]
        return lax.dot_general(q, kj, (((1,), (1,)), ((), ())), preferred_element_type=F32)

    def update(sidx, hh, j, s, masked):
        off = pl.multiple_of(j * ATT_TK, ATT_TK)
        vj = v_ref[pl.ds(off, ATT_TK), V_DIM * hh: V_DIM * (hh + 1)]
        if masked:
            row = lax.broadcasted_iota(jnp.int32, s.shape, 0)
            col = lax.broadcasted_iota(jnp.int32, s.shape, 1)
            s = jnp.where(col <= row, s, NEG)
        m_prev = m_ref[sidx]
        m_new = jnp.maximum(m_prev, jnp.max(s, axis=-1, keepdims=True))
        alpha = jnp.exp2(m_prev - m_new)
        p = jnp.exp2(s - jnp.concatenate([m_new] * nrep, axis=1))
        psum = p[:, :128]
        for c in range(1, nrep):
            psum = psum + p[:, 128 * c: 128 * (c + 1)]
        l_ref[sidx] = alpha * l_ref[sidx] + psum
        acc_ref[sidx] = alpha * acc_ref[sidx] + jnp.dot(p.astype(BF), vj, preferred_element_type=F32)
        m_ref[sidx] = m_new

    def run(items, masked):
        ss = [scores(hh, j) for (_, hh, j) in items]
        for (sidx, hh, j), s in zip(items, ss):
            update(sidx, hh, j, s, masked)

    def body(t, carry):
        run([(2 * hh + e, hh, 2 * t + e) for hh in range(ATT_HG) for e in range(2)], False)
        return carry

    lax.fori_loop(0, lax.shift_right_logical(qi, 1), body, 0)

    @pl.when(jnp.bitwise_and(qi, 1) == 1)
    def _():
        run([(2 * hh, hh, qi - 1) for hh in range(ATT_HG)], False)

    run([(2 * hh + 1, hh, qi) for hh in range(ATT_HG)], True)

    for hh in range(ATT_HG):
        m0 = m_ref[2 * hh]
        m1 = m_ref[2 * hh + 1]
        m = jnp.maximum(m0, m1)
        a0 = jnp.exp2(m0 - m)
        a1 = jnp.exp2(m1 - m)
        l = jnp.sum(a0 * l_ref[2 * hh] + a1 * l_ref[2 * hh + 1], axis=-1, keepdims=True)
        acc = a0 * acc_ref[2 * hh] + a1 * acc_ref[2 * hh + 1]
        g = gate_ref[:, V_DIM * hh: V_DIM * (hh + 1)].astype(F32)
        o_ref[:, V_DIM * hh: V_DIM * (hh + 1)] = (acc * (1.0 / l) * (g * jax.nn.sigmoid(g))).astype(BF)


def _mla_attn(qp, kp, vp, p, batch, seq):
    n = qp.shape[0]
    nq = seq // ATT_TQ
    hg = ATT_HG
    return pl.pallas_call(
        _attn_kernel,
        grid=(batch, MLA_HEADS // hg, nq),
        in_specs=[
            pl.BlockSpec((ATT_TQ, hg * HEAD_SLAB), lambda b, h, i: (b * nq + i, h)),
            pl.BlockSpec((seq, hg * HEAD_SLAB), lambda b, h, i: (b, h)),
            pl.BlockSpec((seq, hg * V_DIM), lambda b, h, i: (b, h)),
            pl.BlockSpec((ATT_TQ, hg * V_DIM), lambda b, h, i: (b * nq + i, P_MG // (hg * V_DIM) + h)),
        ],
        out_specs=pl.BlockSpec((ATT_TQ, hg * V_DIM), lambda b, h, i: (b * nq + i, h)),
        out_shape=jax.ShapeDtypeStruct((n, MLA_HEADS * V_DIM), BF),
        scratch_shapes=[
            pltpu.VMEM((ATT_NS, ATT_TQ, 128), F32),
            pltpu.VMEM((ATT_NS, ATT_TQ, 128), F32),
            pltpu.VMEM((ATT_NS, ATT_TQ, V_DIM), F32),
        ],
        compiler_params=_cparams(("parallel", "parallel", "arbitrary")),
        name="mla_attn",
    )(qp, kp, vp, p)


GDN_T = 128
GDN_NC = GDN_T // CHUNK


def _split3(x):
    a = x.astype(BF)
    r1 = x - a.astype(F32)
    b = r1.astype(BF)
    c = (r1 - b.astype(F32)).astype(BF)
    return a, b, c


def _dot_nt(a, b):
    return lax.dot_general(a, b, (((1,), (1,)), ((), ())), preferred_element_type=F32)


def _dot_tn(a, b):
    return lax.dot_general(a, b, (((0,), (0,)), ((), ())), preferred_element_type=F32)


def _gdn_kernel(q_ref, k_ref, v_ref, hq_ref, hk_ref, hv_ref, g_ref, gate_ref,
                cw_ref, arow_ref, dtrow_ref, og_ref, o_ref,
                state_ref, xq_ref, xk_ref, xv_ref):
    t = pl.program_id(1)
    T = GDN_T

    @pl.when(t == 0)
    def _():
        state_ref[...] = jnp.zeros(state_ref.shape, F32)

    first = t == 0

    def conv_silu(x_ref, h_ref, xp_ref, col0):
        halo = jnp.where(first, 0.0, h_ref[...].astype(F32))
        xp_ref[0:8, :] = halo
        xp_ref[8:8 + T, :] = x_ref[...].astype(F32)
        y = xp_ref[pl.ds(5, T), :] * cw_ref[0:1, col0:col0 + GDN_W]
        for j in range(1, CONV_W):
            y = y + xp_ref[pl.ds(5 + j, T), :] * cw_ref[j:j + 1, col0:col0 + GDN_W]
        return y * jax.nn.sigmoid(y)

    qc = conv_silu(q_ref, hq_ref, xq_ref, 0)
    kc = conv_silu(k_ref, hk_ref, xk_ref, GDN_W)
    vc = conv_silu(v_ref, hv_ref, xv_ref, 2 * GDN_W)

    gin = g_ref[...]
    gx = gin + dtrow_ref[...]
    softplus = jnp.maximum(gx, 0.0) + jnp.log1p(jnp.exp(-jnp.abs(gx)))
    gdec = -jnp.exp(arow_ref[...]) * softplus
    beta = jax.nn.sigmoid(gin)

    ri = lax.broadcasted_iota(jnp.int32, (T, T), 0)
    ci = lax.broadcasted_iota(jnp.int32, (T, T), 1)
    same_chunk = jnp.right_shift(ri, 6) == jnp.right_shift(ci, 6)
    tri = jnp.where(same_chunk, jnp.where(ci <= ri, 1.0, 0.0), 0.0).astype(BF)
    g1, g2, g3 = _split3(gdec)
    gc = (jnp.dot(tri, g1, preferred_element_type=F32)
          + jnp.dot(tri, g2, preferred_element_type=F32)
          + jnp.dot(tri, g3, preferred_element_type=F32))

    er = lax.broadcasted_iota(jnp.int32, (128, GDN_W), 0)
    ec = lax.broadcasted_iota(jnp.int32, (128, GDN_W), 1)
    e_g = jnp.where(er == jnp.right_shift(ec, 7), 1.0, 0.0).astype(BF)
    e_b = jnp.where(er == jnp.right_shift(ec, 7) + GDN_HEADS, 1.0, 0.0).astype(BF)
    c1, c2, c3 = _split3(gc)
    gcb = (jnp.dot(c1, e_g, preferred_element_type=F32)
           + jnp.dot(c2, e_g, preferred_element_type=F32)
           + jnp.dot(c3, e_g, preferred_element_type=F32))
    b1, b2, _ = _split3(beta)
    betab = (jnp.dot(b1, e_b, preferred_element_type=F32)
             + jnp.dot(b2, e_b, preferred_element_type=F32))
    gct = gc.T

    lane = lax.broadcasted_iota(jnp.int32, (1, 128), 1)
    lm0 = lane < CHUNK
    ii = lax.broadcasted_iota(jnp.int32, (CHUNK, 128), 0)
    jm = jnp.bitwise_and(lax.broadcasted_iota(jnp.int32, (CHUNK, 128), 1), CHUNK - 1)
    low = ii >= jm
    strict = ii > jm
    eye_p = jnp.where(ii == jm, 1.0, 0.0)
    heads = range(GDN_HEADS)

    def bd(xp):
        return jnp.concatenate([jnp.where(lm0, xp, 0.0), jnp.where(lm0, 0.0, xp)], axis=0).astype(BF)

    def mm(a, b):
        return jnp.dot(a, b, preferred_element_type=F32)

    qd, rhs, kdt, egl, kq, dec = [], [], [], [], [], []
    for h in heads:
        hs = slice(GDN_DIM * h, GDN_DIM * (h + 1))
        qh = qc[:, hs]
        kh = kc[:, hs]
        qh = qh * (lax.rsqrt(jnp.sum(qh * qh, axis=-1, keepdims=True) + EPS) * (GDN_DIM ** -0.5))
        kh = kh * lax.rsqrt(jnp.sum(kh * kh, axis=-1, keepdims=True) + EPS)
        bh = betab[:, hs]
        gh = gcb[:, hs]
        egh = jnp.exp(gh)
        kbh = kh * bh
        kt = kh.T
        rowp = gct[h:h + 1, :]
        lhs = jnp.concatenate(
            [jnp.concatenate([kbh[:CHUNK], kbh[CHUNK:]], axis=1),
             jnp.concatenate([qh[:CHUNK], qh[CHUNK:]], axis=1)], axis=0).astype(BF)
        bdt = jnp.concatenate([jnp.where(lm0, kt, 0.0), jnp.where(lm0, 0.0, kt)], axis=0).astype(BF)
        kq.append(mm(lhs, bdt))
        colp = jnp.where(lm0, gh[:CHUNK], gh[CHUNK:])
        dec.append(jnp.where(low, jnp.exp(jnp.where(low, colp - rowp, 0.0)), 0.0))
        rhs.append(jnp.concatenate([vc[:, hs] * bh, kbh * egh], axis=1).astype(BF))
        qd.append((qh * egh).astype(BF))
        gl0 = gh[CHUNK - 1:CHUNK, :]
        gl1 = gh[2 * CHUNK - 1:2 * CHUNK, :]
        rf0 = jnp.where(lm0, jnp.exp(jnp.where(lm0, gl0 - rowp, 0.0)), 0.0)
        rf1 = jnp.where(lm0, 0.0, jnp.exp(jnp.where(lm0, 0.0, gl1 - rowp)))
        kdt.append(((kt * rf0).astype(BF), (kt * rf1).astype(BF)))
        egl.append((jnp.exp(gl0), jnp.exp(gl1)))

    mneg = [jnp.where(strict, -(kq[h][:CHUNK] * dec[h]), 0.0) for h in heads]
    attn = [kq[h][CHUNK:] * dec[h] for h in heads]
    pinv = [eye_p + mneg[h] for h in heads]
    mp = [mm(mneg[h].astype(BF), bd(mneg[h])) for h in heads]
    for _ in range(4):
        r = [mm(jnp.concatenate([pinv[h], mp[h]], axis=0).astype(BF), bd(mp[h])) for h in heads]
        pinv = [pinv[h] + r[h][:CHUNK] for h in heads]
        mp = [r[h][CHUNK:] for h in heads]
    pinv = [pinv[h] + mm(pinv[h].astype(BF), bd(mp[h])) for h in heads]
    sol = [mm(bd(pinv[h]), rhs[h]) for h in heads]

    st = [state_ref[h] for h in heads]
    o_parts = [[] for _ in heads]
    zeros_c = jnp.zeros((CHUNK, GDN_DIM), BF)
    for c in range(GDN_NC):
        rs = slice(CHUNK * c, CHUNK * (c + 1))
        r1 = [mm(jnp.concatenate([sol[h][rs, GDN_DIM:].astype(BF), qd[h][rs]], axis=0), st[h].astype(BF))
              for h in heads]
        for h in heads:
            vn = (sol[h][rs, :GDN_DIM] - r1[h][:CHUNK]).astype(BF)
            vfull = jnp.concatenate([vn, zeros_c] if c == 0 else [zeros_c, vn], axis=0)
            am = jnp.where(lm0, attn[h], 0.0) if c == 0 else jnp.where(lm0, 0.0, attn[h])
            r2 = mm(jnp.concatenate([am.astype(BF), kdt[h][c]], axis=0), vfull)
            o_parts[h].append(r1[h][CHUNK:] + r2[:CHUNK])
            st[h] = st[h] * egl[h][c] + r2[CHUNK:]
    for h in heads:
        hs = slice(GDN_DIM * h, GDN_DIM * (h + 1))
        state_ref[h] = st[h]
        o_h = _rms(jnp.concatenate(o_parts[h], axis=0), og_ref[...])
        gt = gate_ref[:, hs].astype(F32)
        o_ref[:, hs] = (o_h * (gt * jax.nn.sigmoid(gt))).astype(BF)


def _gdn(p, g, cw, arow, dtrow, og, batch, seq):
    n = p.shape[0]
    T = GDN_T
    nt = seq // T
    tok = lambda col: pl.BlockSpec((T, GDN_W), lambda b, t: (b * nt + t, col // GDN_W))
    halo = lambda col: pl.BlockSpec(
        (8, GDN_W), lambda b, t: (jnp.maximum((b * nt + t) * (T // 8) - 1, 0), col // GDN_W))
    full = lambda r, c: pl.BlockSpec((r, c), lambda b, t: (0, 0))
    return pl.pallas_call(
        _gdn_kernel,
        grid=(batch, nt),
        in_specs=[
            tok(P_GQ), tok(P_GK), tok(P_GV), halo(P_GQ), halo(P_GK), halo(P_GV),
            pl.BlockSpec((T, 128), lambda b, t: (b * nt + t, 0)),
            tok(P_GG),
            full(CONV_W, 3 * GDN_W), full(1, 128), full(1, 128), full(1, GDN_DIM),
        ],
        out_specs=pl.BlockSpec((T, GDN_W), lambda b, t: (b * nt + t, 0)),
        out_shape=jax.ShapeDtypeStruct((n, GDN_W), BF),
        scratch_shapes=[
            pltpu.VMEM((GDN_HEADS, GDN_DIM, GDN_DIM), F32),
            pltpu.VMEM((T + 8, GDN_W), F32),
            pltpu.VMEM((T + 8, GDN_W), F32),
            pltpu.VMEM((T + 8, GDN_W), F32),
        ],
        compiler_params=_cparams(("parallel", "arbitrary")),
        name="gdn",
    )(p, p, p, p, p, p, g, p, cw, arow, dtrow, og)


OUT_TM, OUT_TN = 1024, 1024


def _outproj_kernel(a_ref, b_ref, wa_ref, wb_ref, x_ref, o_ref):
    acc = jnp.dot(a_ref[...], wa_ref[...], preferred_element_type=F32)
    acc = acc + jnp.dot(b_ref[...], wb_ref[...], preferred_element_type=F32)
    o_ref[...] = x_ref[...] + acc


def _out_proj(a, b, wa, wb, x2):
    n = x2.shape[0]
    half = a.shape[1]
    return pl.pallas_call(
        _outproj_kernel,
        grid=(n // OUT_TM, D_MODEL // OUT_TN),
        in_specs=[
            pl.BlockSpec((OUT_TM, half), lambda i, j: (i, 0)),
            pl.BlockSpec((OUT_TM, half), lambda i, j: (i, 0)),
            pl.BlockSpec((half, OUT_TN), lambda i, j: (0, j)),
            pl.BlockSpec((half, OUT_TN), lambda i, j: (0, j)),
            pl.BlockSpec((OUT_TM, OUT_TN), lambda i, j: (i, j)),
        ],
        out_specs=pl.BlockSpec((OUT_TM, OUT_TN), lambda i, j: (i, j)),
        out_shape=jax.ShapeDtypeStruct((n, D_MODEL), F32),
        compiler_params=_cparams(("parallel", "arbitrary")),
        name="out_proj",
    )(a, b, wa, wb, x2)


def _rot_cols(w):
    return jnp.concatenate([-w[..., ROPE // 2:], w[..., :ROPE // 2]], axis=-1)


def _swap_halves(g):
    return jnp.concatenate([g[..., ROPE // 2:], g[..., :ROPE // 2]], axis=-1)


def _prep_w_in(w):
    cq, ckv, kr, mg, gq, gk, gv, ga, gb, gg = jnp.split(
        w, [512, 768, 832, 1856, 2880, 3904, 4928, 4936, 4944], axis=1)
    pad = jnp.zeros((w.shape[0], P_COLS - P_G - 16), w.dtype)
    return jnp.concatenate([gq, gk, gv, gg, mg, cq, ckv, kr, _rot_cols(kr), ga, gb, pad],
                           axis=1).astype(BF)


def _prep_w_uq(w):
    w3 = w.reshape(Q_LORA, MLA_HEADS, QK_DIM)
    rope = w3[..., NOPE:]
    return jnp.concatenate([w3, _rot_cols(rope)], axis=-1).reshape(Q_LORA, MLA_HEADS * HEAD_SLAB).astype(BF)


def _prep_qk_gain(g):
    rope = g[NOPE:]
    return jnp.concatenate([g, _swap_halves(rope)])[None, :]


def kernel(x, positions, norm_gain, w_in, mla_q_a_gain, mla_kv_a_gain, w_uq, w_ukv,
           mla_q_norm_gain, mla_k_norm_gain, gdn_conv_w, gdn_a_log, gdn_dt_bias,
           gdn_out_norm_gain, w_out):
    batch, seq, _ = x.shape
    n = batch * seq
    x2 = x.reshape(n, D_MODEL)
    pos = positions.reshape(n, 1)
    half = ROPE // 2
    inv_freq = jnp.power(ROPE_THETA, -jnp.arange(half, dtype=F32) / half)
    invf = jnp.tile(inv_freq, 4)[None, :]

    h = x2
    for layer in range(w_in.shape[0]):
        p, g = _in_proj(h, norm_gain[layer][None, :], _prep_w_in(w_in[layer]))
        qp, kp, vp = _mla_prep(
            p, pos, mla_q_a_gain[layer][None, :], mla_kv_a_gain[layer][None, :],
            _prep_w_uq(w_uq[layer]), w_ukv[layer].astype(BF),
            _prep_qk_gain(mla_q_norm_gain[layer]), _prep_qk_gain(mla_k_norm_gain[layer]), invf)
        o_mla = _mla_attn(qp, kp, vp, p, batch, seq)
        zpad = jnp.zeros((128 - GDN_HEADS,), F32)
        arow = jnp.concatenate([gdn_a_log[layer].astype(F32), zpad])[None, :]
        dtrow = jnp.concatenate([gdn_dt_bias[layer].astype(F32), zpad])[None, :]
        o_gdn = _gdn(p, g, gdn_conv_w[layer], arow, dtrow, gdn_out_norm_gain[layer][None, :], batch, seq)
        wo = w_out[layer].astype(BF)
        h = _out_proj(o_mla, o_gdn, wo[:MLA_HEADS * V_DIM], wo[MLA_HEADS * V_DIM:], h)
    return h.reshape(batch, seq, D_MODEL)
```

```python
import functools
import math

import jax
import jax.numpy as jnp
from jax import lax
from jax.experimental import pallas as pl
from jax.experimental.pallas import tpu as pltpu

BF = jnp.bfloat16
F32 = jnp.float32

D_MODEL = 2048
MLA_HEADS = 8
NOPE = 128
ROPE = 64
QK_DIM = NOPE + ROPE
V_DIM = 128
Q_LORA = 512
KV_LORA = 256
GDN_HEADS = 8
GDN_DIM = 128
GDN_W = GDN_HEADS * GDN_DIM
CONV_W = 4
CHUNK = 64
ROPE_THETA = 10000.0
EPS = 1e-6
LOG2E = math.log2(math.e)

P_GQ, P_GK, P_GV, P_GG, P_MG = 0, 1024, 2048, 3072, 4096
P_CQ, P_CKV, P_KR, P_G = 5120, 5632, 5888, 6016
P_COLS = 6144

VMEM_LIMIT = 56 * 1024 * 1024


def _cparams(sem):
    return pltpu.CompilerParams(dimension_semantics=sem, vmem_limit_bytes=VMEM_LIMIT)


IN_TM, IN_TN = 1024, 768


def _inproj_kernel(x_ref, gain_ref, w_ref, p_ref, g_ref, xn_ref):
    j = pl.program_id(1)

    @pl.when(j == 0)
    def _():
        xf = x_ref[...]
        ms = jnp.mean(xf * xf, axis=-1, keepdims=True)
        xn_ref[...] = (xf * lax.rsqrt(ms + EPS) * gain_ref[...]).astype(BF)

    acc = jnp.dot(xn_ref[...], w_ref[...], preferred_element_type=F32)
    p_ref[...] = acc.astype(BF)

    @pl.when(j == pl.num_programs(1) - 1)
    def _():
        g_ref[...] = acc[:, IN_TN - 128:]


def _in_proj(x2, gain, w):
    n = x2.shape[0]
    return pl.pallas_call(
        _inproj_kernel,
        grid=(n // IN_TM, P_COLS // IN_TN),
        in_specs=[
            pl.BlockSpec((IN_TM, D_MODEL), lambda i, j: (i, 0)),
            pl.BlockSpec((1, D_MODEL), lambda i, j: (0, 0)),
            pl.BlockSpec((D_MODEL, IN_TN), lambda i, j: (0, j)),
        ],
        out_specs=[
            pl.BlockSpec((IN_TM, IN_TN), lambda i, j: (i, j)),
            pl.BlockSpec((IN_TM, 128), lambda i, j: (i, 0)),
        ],
        out_shape=[
            jax.ShapeDtypeStruct((n, P_COLS), BF),
            jax.ShapeDtypeStruct((n, 128), F32),
        ],
        scratch_shapes=[pltpu.VMEM((IN_TM, D_MODEL), BF)],
        compiler_params=_cparams(("parallel", "arbitrary")),
        name="in_proj",
    )(x2, gain, w)


PREP_TM = 512
HEAD_SLAB = 256


def _rms(x, gain):
    ms = jnp.mean(x * x, axis=-1, keepdims=True)
    return x * lax.rsqrt(ms + EPS) * gain


def _mla_prep_kernel(cq_ref, ckv_ref, kr_ref, pos_ref, qag_ref, kvag_ref, wuq_ref, wukv_ref,
                     qg_ref, kg_ref, invf_ref, q_ref, k_ref, v_ref):
    lane = lax.broadcasted_iota(jnp.int32, (1, 128), 1)
    lo64 = lane < ROPE

    ang = pos_ref[...].astype(F32) * invf_ref[...]
    cs = jnp.cos(ang - jnp.where(lo64, 0.0, 0.5 * math.pi))

    cqn = _rms(cq_ref[...].astype(F32), qag_ref[...]).astype(BF)
    y = jnp.dot(cqn, wuq_ref[...], preferred_element_type=F32)
    qg0 = qg_ref[:, :128]
    qg1 = qg_ref[:, 128:]
    qscale = QK_DIM ** -0.5 * LOG2E
    for h in range(MLA_HEADS):
        y0 = y[:, HEAD_SLAB * h: HEAD_SLAB * h + 128]
        y1 = y[:, HEAD_SLAB * h + 128: HEAD_SLAB * (h + 1)]
        ss = (jnp.sum(y0 * y0, axis=-1, keepdims=True)
              + jnp.sum(jnp.where(lo64, y1 * y1, 0.0), axis=-1, keepdims=True))
        r = lax.rsqrt(ss * (1.0 / QK_DIM) + EPS)
        qn = y0 * r * qg0
        z = y1 * r * qg1 * cs
        qr = z + pltpu.roll(z, ROPE, axis=1)
        q_ref[:, HEAD_SLAB * h: HEAD_SLAB * h + 128] = (qn * qscale).astype(BF)
        q_ref[:, HEAD_SLAB * h + 128: HEAD_SLAB * (h + 1)] = (qr * qscale).astype(BF)

    ckvn = _rms(ckv_ref[...].astype(F32), kvag_ref[...]).astype(BF)
    kv = jnp.dot(ckvn, wukv_ref[...], preferred_element_type=F32)
    kr = kr_ref[...].astype(F32)
    ssr = jnp.sum(jnp.where(lo64, kr * kr, 0.0), axis=-1, keepdims=True)
    kg0 = kg_ref[:, :128]
    kg1 = kg_ref[:, 128:]
    zk0 = kr * kg1 * cs
    for h in range(MLA_HEADS):
        kn = kv[:, HEAD_SLAB * h: HEAD_SLAB * h + 128]
        ss = jnp.sum(kn * kn, axis=-1, keepdims=True) + ssr
        r = lax.rsqrt(ss * (1.0 / QK_DIM) + EPS)
        zk = zk0 * r
        krh = jnp.where(lo64, zk + pltpu.roll(zk, ROPE, axis=1), 0.0)
        k_ref[:, HEAD_SLAB * h: HEAD_SLAB * h + 128] = (kn * r * kg0).astype(BF)
        k_ref[:, HEAD_SLAB * h + 128: HEAD_SLAB * (h + 1)] = krh.astype(BF)
        v_ref[:, V_DIM * h: V_DIM * (h + 1)] = kv[:, HEAD_SLAB * h + 128: HEAD_SLAB * (h + 1)].astype(BF)


def _mla_prep(p, pos, qag, kvag, wuq, wukv, qg, kg, invf):
    n = p.shape[0]
    tm = PREP_TM
    full = lambda r, c: pl.BlockSpec((r, c), lambda i: (0, 0))
    return pl.pallas_call(
        _mla_prep_kernel,
        grid=(n // tm,),
        in_specs=[
            pl.BlockSpec((tm, Q_LORA), lambda i: (i, P_CQ // Q_LORA)),
            pl.BlockSpec((tm, KV_LORA), lambda i: (i, P_CKV // KV_LORA)),
            pl.BlockSpec((tm, 128), lambda i: (i, P_KR // 128)),
            pl.BlockSpec((tm, 1), lambda i: (i, 0)),
            full(1, Q_LORA), full(1, KV_LORA),
            full(Q_LORA, MLA_HEADS * HEAD_SLAB), full(KV_LORA, MLA_HEADS * HEAD_SLAB),
            full(1, HEAD_SLAB), full(1, HEAD_SLAB), full(1, 128),
        ],
        out_specs=[
            pl.BlockSpec((tm, MLA_HEADS * HEAD_SLAB), lambda i: (i, 0)),
            pl.BlockSpec((tm, MLA_HEADS * HEAD_SLAB), lambda i: (i, 0)),
            pl.BlockSpec((tm, MLA_HEADS * V_DIM), lambda i: (i, 0)),
        ],
        out_shape=[
            jax.ShapeDtypeStruct((n, MLA_HEADS * HEAD_SLAB), BF),
            jax.ShapeDtypeStruct((n, MLA_HEADS * HEAD_SLAB), BF),
            jax.ShapeDtypeStruct((n, MLA_HEADS * V_DIM), BF),
        ],
        compiler_params=_cparams(("parallel",)),
        name="mla_prep",
    )(p, p, p, pos, qag, kvag, wuq, wukv, qg, kg, invf)


ATT_TQ = 512
ATT_TK = 512
ATT_HG = 4
ATT_NS = ATT_HG
NEG = -1e30


def _attn_kernel(q_ref, k_ref, v_ref, gate_ref, o_ref, m_ref, acc_ref):
    qi = pl.program_id(2)
    m_ref[...] = jnp.full(m_ref.shape, NEG, F32)
    acc_ref[...] = jnp.zeros(acc_ref.shape, F32)
    nrep = ATT_TK // 128
    ones_col = jnp.where(lax.broadcasted_iota(jnp.int32, (ATT_TK, 128), 1) == 0, 1.0, 0.0).astype(BF)

    def scores(hh, j):
        off = pl.multiple_of(j * ATT_TK, ATT_TK)
        q = q_ref[:, HEAD_SLAB * hh: HEAD_SLAB * (hh + 1)]
        kj = k_ref[pl.ds(off, ATT_TK), HEAD_SLAB * hh: HEAD_SLAB * (hh + 1)]
        return lax.dot_general(q, kj, (((1,), (1,)), ((), ())), preferred_element_type=F32)

    def update(sidx, hh, j, s, masked):
        off = pl.multiple_of(j * ATT_TK, ATT_TK)
        vj = v_ref[pl.ds(off, ATT_TK), V_DIM * hh: V_DIM * (hh + 1)]
        if masked:
            row = lax.broadcasted_iota(jnp.int32, s.shape, 0)
            col = lax.broadcasted_iota(jnp.int32, s.shape, 1)
            s = jnp.where(col <= row, s, NEG)
        m_prev = m_ref[sidx]
        m_new = jnp.maximum(m_prev, jnp.max(s, axis=-1, keepdims=True))
        alpha = jnp.exp2(m_prev - m_new)
        p = jnp.exp2((s - jnp.concatenate([m_new] * nrep, axis=1)).astype(BF))
        pv = jnp.dot(p, jnp.concatenate([vj, ones_col], axis=1), preferred_element_type=F32)
        acc_ref[sidx] = jnp.concatenate([alpha, alpha], axis=1) * acc_ref[sidx] + pv
        m_ref[sidx] = m_new

    def run(items, masked):
        ss = [scores(hh, j) for (_, hh, j) in items]
        for (sidx, hh, j), s in zip(items, ss):
            update(sidx, hh, j, s, masked)

    def body(j, carry):
        run([(hh, hh, j) for hh in range(ATT_HG)], False)
        return carry

    lax.fori_loop(0, qi, body, 0)
    run([(hh, hh, qi) for hh in range(ATT_HG)], True)

    for hh in range(ATT_HG):
        acc = acc_ref[hh]
        l = jnp.sum(acc[:, V_DIM:], axis=-1, keepdims=True)
        g = gate_ref[:, V_DIM * hh: V_DIM * (hh + 1)].astype(F32)
        o_ref[:, V_DIM * hh: V_DIM * (hh + 1)] = (
            acc[:, :V_DIM] * (1.0 / l) * (g * jax.nn.sigmoid(g))).astype(BF)


def _mla_attn(qp, kp, vp, p, batch, seq):
    n = qp.shape[0]
    nq = seq // ATT_TQ
    hg = ATT_HG
    return pl.pallas_call(
        _attn_kernel,
        grid=(batch, MLA_HEADS // hg, nq),
        in_specs=[
            pl.BlockSpec((ATT_TQ, hg * HEAD_SLAB), lambda b, h, i: (b * nq + i, h)),
            pl.BlockSpec((seq, hg * HEAD_SLAB), lambda b, h, i: (b, h), pipeline_mode=pl.Buffered(1)),
            pl.BlockSpec((seq, hg * V_DIM), lambda b, h, i: (b, h)),
            pl.BlockSpec((ATT_TQ, hg * V_DIM), lambda b, h, i: (b * nq + i, P_MG // (hg * V_DIM) + h)),
        ],
        out_specs=pl.BlockSpec((ATT_TQ, hg * V_DIM), lambda b, h, i: (b * nq + i, h)),
        out_shape=jax.ShapeDtypeStruct((n, MLA_HEADS * V_DIM), BF),
        scratch_shapes=[
            pltpu.VMEM((ATT_NS, ATT_TQ, 128), F32),
            pltpu.VMEM((ATT_NS, ATT_TQ, 2 * V_DIM), F32),
        ],
        compiler_params=_cparams(("parallel", "parallel", "arbitrary")),
        name="mla_attn",
    )(qp, kp, vp, p)


GDN_T = 256
GDN_NB = GDN_T // 128
GDN_LOG2_T = GDN_T.bit_length() - 1


def _split3(x):
    a = x.astype(BF)
    r1 = x - a.astype(F32)
    b = r1.astype(BF)
    c = (r1 - b.astype(F32)).astype(BF)
    return a, b, c


def _dot_nt(a, b):
    return lax.dot_general(a, b, (((1,), (1,)), ((), ())), preferred_element_type=F32)


def _dot_tn(a, b):
    return lax.dot_general(a, b, (((0,), (0,)), ((), ())), preferred_element_type=F32)


def _gdn_kernel(q_ref, k_ref, v_ref, hq_ref, hk_ref, hv_ref, g_ref, gate_ref,
                cw_ref, arow_ref, dtrow_ref, og_ref, o_ref,
                state_ref):
    t = pl.program_id(1)
    T = GDN_T

    @pl.when(t == 0)
    def _():
        state_ref[...] = jnp.zeros(state_ref.shape, F32)

    first = t == 0

    sr = lax.broadcasted_iota(jnp.int32, ((CONV_W - 1) * 128, 128), 0)
    sc = lax.broadcasted_iota(jnp.int32, ((CONV_W - 1) * 128, 128), 1)
    shift_all = jnp.where(sc == jnp.bitwise_and(sr, 127) - (CONV_W - 1) + jnp.right_shift(sr, 7),
                          1.0, 0.0).astype(BF)
    row8 = lax.broadcasted_iota(jnp.int32, (8, 1), 0)

    def conv_silu(x_ref, h_ref, col0):
        w = [cw_ref[j:j + 1, col0:col0 + GDN_W] for j in range(CONV_W)]
        hz = jnp.where(first, 0.0, h_ref[...].astype(F32))
        ys = []
        for blk in range(GDN_NB):
            xb = x_ref[128 * blk:128 * (blk + 1), :]
            xf = xb.astype(F32)
            xs = jnp.dot(shift_all, xb, preferred_element_type=F32)
            y = xf * w[CONV_W - 1]
            corr = jnp.zeros((8, GDN_W), F32)
            for j in range(CONV_W - 1):
                y = y + xs[j * 128:(j + 1) * 128] * w[j]
                sh = CONV_W - 1 - j
                corr = corr + jnp.where(row8 < sh, pltpu.roll(hz, sh, axis=0), 0.0) * w[j]
            ys += [y[:8] + corr, y[8:]]
            hz = xf[120:]
        y = jnp.concatenate(ys, axis=0)
        return y * jax.nn.sigmoid(y)

    qc = conv_silu(q_ref, hq_ref, 0)
    kc = conv_silu(k_ref, hk_ref, GDN_W)
    vc = conv_silu(v_ref, hv_ref, 2 * GDN_W)

    gin = g_ref[...]
    gx = gin + dtrow_ref[...]
    softplus = jnp.maximum(gx, 0.0) + jnp.log1p(jnp.exp(-jnp.abs(gx)))
    gdec = -jnp.exp(arow_ref[...]) * softplus
    beta = jax.nn.sigmoid(gin)

    ri = lax.broadcasted_iota(jnp.int32, (T, T), 0)
    ci = lax.broadcasted_iota(jnp.int32, (T, T), 1)
    same_chunk = jnp.right_shift(ri, 6) == jnp.right_shift(ci, 6)
    tri = jnp.where(same_chunk, jnp.where(ci <= ri, 1.0, 0.0), 0.0).astype(BF)
    g1, g2, g3 = _split3(gdec)
    gc = (jnp.dot(tri, g1, preferred_element_type=F32)
          + jnp.dot(tri, g2, preferred_element_type=F32)
          + jnp.dot(tri, g3, preferred_element_type=F32))

    er = lax.broadcasted_iota(jnp.int32, (128, GDN_W), 0)
    ec = lax.broadcasted_iota(jnp.int32, (128, GDN_W), 1)
    e_g = jnp.where(er == jnp.right_shift(ec, 7), 1.0, 0.0).astype(BF)
    e_b = jnp.where(er == jnp.right_shift(ec, 7) + GDN_HEADS, 1.0, 0.0).astype(BF)
    c1, c2, _ = _split3(gc)
    gcb = (jnp.dot(c1, e_g, preferred_element_type=F32)
           + jnp.dot(c2, e_g, preferred_element_type=F32))
    betab = jnp.dot(beta.astype(BF), e_b, preferred_element_type=F32)
    gct = gc.T

    lane = lax.broadcasted_iota(jnp.int32, (1, 128), 1)
    lm0 = lane < CHUNK
    ii = lax.broadcasted_iota(jnp.int32, (CHUNK, 128), 0)
    jm = jnp.bitwise_and(lax.broadcasted_iota(jnp.int32, (CHUNK, 128), 1), CHUNK - 1)
    low = ii >= jm
    strict = ii > jm
    eye_p = jnp.where(ii == jm, 1.0, 0.0)
    heads = range(GDN_HEADS)
    items = [(blk, h) for blk in range(GDN_NB) for h in heads]
    nit = range(len(items))

    def bd(xp):
        return jnp.concatenate([jnp.where(lm0, xp, 0.0), jnp.where(lm0, 0.0, xp)], axis=0).astype(BF)

    def mm(a, b):
        return jnp.dot(a, b, preferred_element_type=F32)

    qd, rhs, kdt, egl, kq, dec = [], [], [], [], [], []
    for blk, h in items:
        hs = slice(GDN_DIM * h, GDN_DIM * (h + 1))
        bs = slice(128 * blk, 128 * (blk + 1))
        qh = qc[bs, hs]
        kh = kc[bs, hs]
        qss = jnp.broadcast_to(jnp.sum(qh * qh, axis=-1, keepdims=True), qh.shape)
        kss = jnp.broadcast_to(jnp.sum(kh * kh, axis=-1, keepdims=True), kh.shape)
        qh = qh * (lax.rsqrt(qss + EPS) * (GDN_DIM ** -0.5))
        kh = kh * lax.rsqrt(kss + EPS)
        bh = betab[bs, hs]
        gh = gcb[bs, hs]
        egh = jnp.exp(gh)
        kbh = kh * bh
        kt = kh.T
        rowp = gct[h:h + 1, bs]
        lhs = jnp.concatenate(
            [jnp.concatenate([kbh[:CHUNK], kbh[CHUNK:]], axis=1),
             jnp.concatenate([qh[:CHUNK], qh[CHUNK:]], axis=1)], axis=0).astype(BF)
        bdt = jnp.concatenate([jnp.where(lm0, kt, 0.0), jnp.where(lm0, 0.0, kt)], axis=0).astype(BF)
        kq.append(mm(lhs, bdt))
        colp = jnp.where(lm0, gh[:CHUNK], gh[CHUNK:])
        dec.append(jnp.where(low, jnp.exp(jnp.where(low, colp - rowp, 0.0)), 0.0))
        rhs.append(jnp.concatenate([vc[bs, hs] * bh, kbh * egh], axis=1).astype(BF))
        qd.append((qh * egh).astype(BF))
        gl0 = gh[CHUNK - 1:CHUNK, :]
        gl1 = gh[2 * CHUNK - 1:2 * CHUNK, :]
        rf0 = jnp.where(lm0, jnp.exp(jnp.where(lm0, gl0 - rowp, 0.0)), 0.0)
        rf1 = jnp.where(lm0, 0.0, jnp.exp(jnp.where(lm0, 0.0, gl1 - rowp)))
        kdt.append(((kt * rf0).astype(BF), (kt * rf1).astype(BF)))
        egl.append((jnp.exp(gl0), jnp.exp(gl1)))

    mneg = [jnp.where(strict, -(kq[i][:CHUNK] * dec[i]), 0.0) for i in nit]
    attn = [kq[i][CHUNK:] * dec[i] for i in nit]
    pinv = [eye_p + mneg[i] for i in nit]
    mp = [mm(mneg[i].astype(BF), bd(mneg[i])) for i in nit]
    for _ in range(4):
        r = [mm(jnp.concatenate([pinv[i], mp[i]], axis=0).astype(BF), bd(mp[i])) for i in nit]
        pinv = [pinv[i] + r[i][:CHUNK] for i in nit]
        mp = [r[i][CHUNK:] for i in nit]
    pinv = [pinv[i] + mm(pinv[i].astype(BF), bd(mp[i])) for i in nit]
    sol = [mm(bd(pinv[i]), rhs[i]) for i in nit]

    st = [state_ref[h] for h in heads]
    o_parts = [[] for _ in heads]
    zeros_c = jnp.zeros((CHUNK, GDN_DIM), BF)
    for blk in range(GDN_NB):
        for c in range(2):
            rs = slice(CHUNK * c, CHUNK * (c + 1))
            r1 = [mm(jnp.concatenate([sol[blk * GDN_HEADS + h][rs, GDN_DIM:].astype(BF),
                                      qd[blk * GDN_HEADS + h][rs]], axis=0), st[h].astype(BF))
                  for h in heads]
            for h in heads:
                i = blk * GDN_HEADS + h
                vn = (sol[i][rs, :GDN_DIM] - r1[h][:CHUNK]).astype(BF)
                vfull = jnp.concatenate([vn, zeros_c] if c == 0 else [zeros_c, vn], axis=0)
                am = jnp.where(lm0, attn[i], 0.0) if c == 0 else jnp.where(lm0, 0.0, attn[i])
                r2 = mm(jnp.concatenate([am.astype(BF), kdt[i][c]], axis=0), vfull)
                o_parts[h].append(r1[h][CHUNK:] + r2[:CHUNK])
                st[h] = st[h] * egl[i][c] + r2[CHUNK:]
    for h in heads:
        hs = slice(GDN_DIM * h, GDN_DIM * (h + 1))
        state_ref[h] = st[h]
        o_h = _rms(jnp.concatenate(o_parts[h], axis=0), og_ref[...])
        gt = gate_ref[:, hs].astype(F32)
        o_ref[:, hs] = (o_h * (gt * jax.nn.sigmoid(gt))).astype(BF)


def _gdn(p, g, cw, arow, dtrow, og, batch, seq):
    n = p.shape[0]
    T = GDN_T
    nt = seq // T
    tok = lambda col: pl.BlockSpec((T, GDN_W), lambda b, t: (b * nt + t, col // GDN_W))
    halo = lambda col: pl.BlockSpec(
        (8, GDN_W), lambda b, t: (jnp.maximum((b * nt + t) * (T // 8) - 1, 0), col // GDN_W))
    full = lambda r, c: pl.BlockSpec((r, c), lambda b, t: (0, 0))
    return pl.pallas_call(
        _gdn_kernel,
        grid=(batch, nt),
        in_specs=[
            tok(P_GQ), tok(P_GK), tok(P_GV), halo(P_GQ), halo(P_GK), halo(P_GV),
            pl.BlockSpec((T, 128), lambda b, t: (b * nt + t, 0)),
            tok(P_GG),
            full(CONV_W, 3 * GDN_W), full(1, 128), full(1, 128), full(1, GDN_DIM),
        ],
        out_specs=pl.BlockSpec((T, GDN_W), lambda b, t: (b * nt + t, 0)),
        out_shape=jax.ShapeDtypeStruct((n, GDN_W), BF),
        scratch_shapes=[
            pltpu.VMEM((GDN_HEADS, GDN_DIM, GDN_DIM), F32),
        ],
        compiler_params=_cparams(("parallel", "arbitrary")),
        name="gdn",
    )(p, p, p, p, p, p, g, p, cw, arow, dtrow, og)


OUT_TM, OUT_TN = 512, 512


def _outproj_kernel(a_ref, b_ref, wa_ref, wb_ref, x_ref, o_ref):
    a = a_ref[...]
    b = b_ref[...]
    for c in range(D_MODEL // OUT_TN):
        cs = slice(OUT_TN * c, OUT_TN * (c + 1))
        acc = jnp.dot(a, wa_ref[:, cs], preferred_element_type=F32)
        acc = acc + jnp.dot(b, wb_ref[:, cs], preferred_element_type=F32)
        o_ref[:, cs] = x_ref[:, cs] + acc


def _out_proj(a, b, wa, wb, x2):
    n = x2.shape[0]
    half = a.shape[1]
    return pl.pallas_call(
        _outproj_kernel,
        grid=(n // OUT_TM,),
        in_specs=[
            pl.BlockSpec((OUT_TM, half), lambda i: (i, 0)),
            pl.BlockSpec((OUT_TM, half), lambda i: (i, 0)),
            pl.BlockSpec((half, D_MODEL), lambda i: (0, 0)),
            pl.BlockSpec((half, D_MODEL), lambda i: (0, 0)),
            pl.BlockSpec((OUT_TM, D_MODEL), lambda i: (i, 0)),
        ],
        out_specs=pl.BlockSpec((OUT_TM, D_MODEL), lambda i: (i, 0)),
        out_shape=jax.ShapeDtypeStruct((n, D_MODEL), F32),
        compiler_params=_cparams(("parallel",)),
        name="out_proj",
    )(a, b, wa, wb, x2)


def _rot_cols(w):
    return jnp.concatenate([-w[..., ROPE // 2:], w[..., :ROPE // 2]], axis=-1)


def _swap_halves(g):
    return jnp.concatenate([g[..., ROPE // 2:], g[..., :ROPE // 2]], axis=-1)


def _prep_w_in(w):
    cq, ckv, kr, mg, gq, gk, gv, ga, gb, gg = jnp.split(
        w, [512, 768, 832, 1856, 2880, 3904, 4928, 4936, 4944], axis=1)
    pad = jnp.zeros((w.shape[0], P_COLS - P_G - 16), w.dtype)
    return jnp.concatenate([gq, gk, gv, gg, mg, cq, ckv, kr, _rot_cols(kr), ga, gb, pad],
                           axis=1).astype(BF)


def _prep_w_uq(w):
    w3 = w.reshape(Q_LORA, MLA_HEADS, QK_DIM)
    rope = w3[..., NOPE:]
    return jnp.concatenate([w3, _rot_cols(rope)], axis=-1).reshape(Q_LORA, MLA_HEADS * HEAD_SLAB).astype(BF)


def _prep_qk_gain(g):
    rope = g[NOPE:]
    return jnp.concatenate([g, _swap_halves(rope)])[None, :]


def kernel(x, positions, norm_gain, w_in, mla_q_a_gain, mla_kv_a_gain, w_uq, w_ukv,
           mla_q_norm_gain, mla_k_norm_gain, gdn_conv_w, gdn_a_log, gdn_dt_bias,
           gdn_out_norm_gain, w_out):
    batch, seq, _ = x.shape
    n = batch * seq
    x2 = x.reshape(n, D_MODEL)
    pos = positions.reshape(n, 1)
    half = ROPE // 2
    inv_freq = jnp.power(ROPE_THETA, -jnp.arange(half, dtype=F32) / half)
    invf = jnp.tile(inv_freq, 4)[None, :]

    h = x2
    for layer in range(w_in.shape[0]):
        p, g = _in_proj(h, norm_gain[layer][None, :], _prep_w_in(w_in[layer]))
        qp, kp, vp = _mla_prep(
            p, pos, mla_q_a_gain[layer][None, :], mla_kv_a_gain[layer][None, :],
            _prep_w_uq(w_uq[layer]), w_ukv[layer].astype(BF),
            _prep_qk_gain(mla_q_norm_gain[layer]), _prep_qk_gain(mla_k_norm_gain[layer]), invf)
        o_mla = _mla_attn(qp, kp, vp, p, batch, seq)
        zpad = jnp.zeros((128 - GDN_HEADS,), F32)
        arow = jnp.concatenate([gdn_a_log[layer].astype(F32), zpad])[None, :]
        dtrow = jnp.concatenate([gdn_dt_bias[layer].astype(F32), zpad])[None, :]
        o_gdn = _gdn(p, g, gdn_conv_w[layer], arow, dtrow, gdn_out_norm_gain[layer][None, :], batch, seq)
        wo = w_out[layer].astype(BF)
        h = _out_proj(o_mla, o_gdn, wo[:MLA_HEADS * V_DIM], wo[MLA_HEADS * V_DIM:], h)
    return h.reshape(batch, seq, D_MODEL)
```

```python
import functools
import math

import jax
import jax.numpy as jnp
from jax import lax
from jax.experimental import pallas as pl
from jax.experimental.pallas import tpu as pltpu

BF = jnp.bfloat16
F32 = jnp.float32

D_MODEL = 2048
MLA_HEADS = 8
NOPE = 128
ROPE = 64
QK_DIM = NOPE + ROPE
V_DIM = 128
Q_LORA = 512
KV_LORA = 256
GDN_HEADS = 8
GDN_DIM = 128
GDN_W = GDN_HEADS * GDN_DIM
CONV_W = 4
CHUNK = 64
ROPE_THETA = 10000.0
EPS = 1e-6
LOG2E = math.log2(math.e)

P_GQ, P_GK, P_GV, P_GG, P_MG = 0, 1024, 2048, 3072, 4096
P_CQ, P_CKV, P_KR, P_G = 5120, 5632, 5888, 6016
P_COLS = 6144

VMEM_LIMIT = 56 * 1024 * 1024


def _cparams(sem):
    return pltpu.CompilerParams(dimension_semantics=sem, vmem_limit_bytes=VMEM_LIMIT)


IN_TM, IN_TN = 1024, 1536
IN_NJ = P_COLS // IN_TN
IN_ROPE_ROWS = IN_TM // IN_NJ


def _inproj_kernel(x_ref, gain_ref, w_ref, pos_ref, invf_ref, p_ref, g_ref, cs_ref, xn_ref):
    j = pl.program_id(1)

    @pl.when(j == 0)
    def _():
        xf = x_ref[...]
        ms = jnp.mean(xf * xf, axis=-1, keepdims=True)
        xn_ref[...] = (xf * lax.rsqrt(ms + EPS) * gain_ref[...]).astype(BF)

    acc = jnp.dot(xn_ref[...], w_ref[...], preferred_element_type=F32)
    p_ref[...] = acc.astype(BF)

    rows = pl.ds(pl.multiple_of(j * IN_ROPE_ROWS, IN_ROPE_ROWS), IN_ROPE_ROWS)
    lane = lax.broadcasted_iota(jnp.int32, (1, 128), 1)
    ang = pos_ref[rows, :].astype(F32) * invf_ref[...]
    cs_ref[rows, :] = jnp.cos(ang - jnp.where(lane < ROPE, 0.0, 0.5 * math.pi))

    @pl.when(j == IN_NJ - 1)
    def _():
        g_ref[...] = acc[:, IN_TN - 128:]


def _in_proj(x2, gain, w, pos, invf):
    n = x2.shape[0]
    return pl.pallas_call(
        _inproj_kernel,
        grid=(n // IN_TM, IN_NJ),
        in_specs=[
            pl.BlockSpec((IN_TM, D_MODEL), lambda i, j: (i, 0)),
            pl.BlockSpec((1, D_MODEL), lambda i, j: (0, 0)),
            pl.BlockSpec((D_MODEL, IN_TN), lambda i, j: (0, j)),
            pl.BlockSpec((IN_TM, 1), lambda i, j: (i, 0)),
            pl.BlockSpec((1, 128), lambda i, j: (0, 0)),
        ],
        out_specs=[
            pl.BlockSpec((IN_TM, IN_TN), lambda i, j: (i, j)),
            pl.BlockSpec((IN_TM, 128), lambda i, j: (i, 0)),
            pl.BlockSpec((IN_TM, 128), lambda i, j: (i, 0)),
        ],
        out_shape=[
            jax.ShapeDtypeStruct((n, P_COLS), BF),
            jax.ShapeDtypeStruct((n, 128), F32),
            jax.ShapeDtypeStruct((n, 128), F32),
        ],
        scratch_shapes=[pltpu.VMEM((IN_TM, D_MODEL), BF)],
        compiler_params=_cparams(("parallel", "arbitrary")),
        name="in_proj",
    )(x2, gain, w, pos, invf)


PREP_TM = 512
HEAD_SLAB = 256


def _rms(x, gain):
    ms = jnp.mean(x * x, axis=-1, keepdims=True)
    return x * lax.rsqrt(ms + EPS) * gain


def _mla_prep_kernel(cq_ref, ckv_ref, kr_ref, cs_ref, qag_ref, kvag_ref, wuq_ref, wukv_ref,
                     qg_ref, kg_ref, q_ref, k_ref, v_ref):
    lane = lax.broadcasted_iota(jnp.int32, (1, 128), 1)
    lo64 = lane < ROPE
    cs = cs_ref[...]

    cqn = _rms(cq_ref[...].astype(F32), qag_ref[...]).astype(BF)
    y = jnp.dot(cqn, wuq_ref[...], preferred_element_type=F32)
    qg0 = qg_ref[:, :128]
    qgcs = qg_ref[:, 128:] * cs
    for h in range(MLA_HEADS):
        y0 = y[:, HEAD_SLAB * h: HEAD_SLAB * h + 128]
        y1 = y[:, HEAD_SLAB * h + 128: HEAD_SLAB * (h + 1)]
        ss = jnp.sum(y0 * y0 + jnp.where(lo64, y1 * y1, 0.0), axis=-1, keepdims=True)
        r = lax.rsqrt(ss * (1.0 / QK_DIM) + EPS)
        q_ref[:, HEAD_SLAB * h: HEAD_SLAB * h + 128] = (y0 * r * qg0).astype(BF)
        q_ref[:, HEAD_SLAB * h + 128: HEAD_SLAB * (h + 1)] = (y1 * r * qgcs).astype(BF)

    ckvn = _rms(ckv_ref[...].astype(F32), kvag_ref[...]).astype(BF)
    kv = jnp.dot(ckvn, wukv_ref[...], preferred_element_type=F32)
    kr = kr_ref[...].astype(F32)
    ssr = jnp.sum(jnp.where(lo64, kr * kr, 0.0), axis=-1, keepdims=True)
    kg0 = kg_ref[:, :128]
    kg1 = kg_ref[:, 128:]
    zk0 = kr * kg1 * cs
    zks = zk0 + pltpu.roll(zk0, ROPE, axis=1)
    for h in range(MLA_HEADS):
        kn = kv[:, HEAD_SLAB * h: HEAD_SLAB * h + 128]
        ss = jnp.sum(kn * kn, axis=-1, keepdims=True) + ssr
        r = lax.rsqrt(ss * (1.0 / QK_DIM) + EPS)
        k_ref[:, HEAD_SLAB * h: HEAD_SLAB * h + 128] = (kn * r * kg0).astype(BF)
        k_ref[:, HEAD_SLAB * h + 128: HEAD_SLAB * (h + 1)] = (zks * r).astype(BF)
        v_ref[:, V_DIM * h: V_DIM * (h + 1)] = kv[:, HEAD_SLAB * h + 128: HEAD_SLAB * (h + 1)].astype(BF)


def _mla_prep(p, cs, qag, kvag, wuq, wukv, qg, kg):
    n = p.shape[0]
    tm = PREP_TM
    full = lambda r, c: pl.BlockSpec((r, c), lambda i: (0, 0))
    return pl.pallas_call(
        _mla_prep_kernel,
        grid=(n // tm,),
        in_specs=[
            pl.BlockSpec((tm, Q_LORA), lambda i: (i, P_CQ // Q_LORA)),
            pl.BlockSpec((tm, KV_LORA), lambda i: (i, P_CKV // KV_LORA)),
            pl.BlockSpec((tm, 128), lambda i: (i, P_KR // 128)),
            pl.BlockSpec((tm, 128), lambda i: (i, 0)),
            full(1, Q_LORA), full(1, KV_LORA),
            full(Q_LORA, MLA_HEADS * HEAD_SLAB), full(KV_LORA, MLA_HEADS * HEAD_SLAB),
            full(1, HEAD_SLAB), full(1, HEAD_SLAB),
        ],
        out_specs=[
            pl.BlockSpec((tm, MLA_HEADS * HEAD_SLAB), lambda i: (i, 0)),
            pl.BlockSpec((tm, MLA_HEADS * HEAD_SLAB), lambda i: (i, 0)),
            pl.BlockSpec((tm, MLA_HEADS * V_DIM), lambda i: (i, 0)),
        ],
        out_shape=[
            jax.ShapeDtypeStruct((n, MLA_HEADS * HEAD_SLAB), BF),
            jax.ShapeDtypeStruct((n, MLA_HEADS * HEAD_SLAB), BF),
            jax.ShapeDtypeStruct((n, MLA_HEADS * V_DIM), BF),
        ],
        compiler_params=_cparams(("parallel",)),
        name="mla_prep",
    )(p, p, p, cs, qag, kvag, wuq, wukv, qg, kg)


ATT_TQ = 512
ATT_TK = 512
ATT_HG = 4
ATT_NS = ATT_HG
NEG = -1e30


def _attn_kernel(q_ref, k_ref, v_ref, gate_ref, o_ref, m_ref, acc_ref):
    qi = pl.program_id(2)
    m_ref[...] = jnp.full(m_ref.shape, NEG, F32)
    acc_ref[...] = jnp.zeros(acc_ref.shape, F32)
    nrep = ATT_TK // 128
    ones_col = jnp.where(lax.broadcasted_iota(jnp.int32, (ATT_TK, 128), 1) == 0, 1.0, 0.0).astype(BF)

    def scores(hh, j):
        off = pl.multiple_of(j * ATT_TK, ATT_TK)
        q = q_ref[:, HEAD_SLAB * hh: HEAD_SLAB * (hh + 1)]
        kj = k_ref[pl.ds(off, ATT_TK), HEAD_SLAB * hh: HEAD_SLAB * (hh + 1)]
        return lax.dot_general(q, kj, (((1,), (1,)), ((), ())), preferred_element_type=F32)

    def update(sidx, hh, j, s, masked):
        off = pl.multiple_of(j * ATT_TK, ATT_TK)
        vj = v_ref[pl.ds(off, ATT_TK), V_DIM * hh: V_DIM * (hh + 1)]
        if masked:
            row = lax.broadcasted_iota(jnp.int32, s.shape, 0) + qi * ATT_TQ
            col = lax.broadcasted_iota(jnp.int32, s.shape, 1) + j * ATT_TK
            s = jnp.where(col <= row, s, NEG)
        m_prev = m_ref[sidx]
        m_new = jnp.maximum(m_prev, jnp.max(s, axis=-1, keepdims=True))
        alpha = jnp.exp2(m_prev - m_new)
        p = jnp.exp2((s - jnp.concatenate([m_new] * nrep, axis=1)).astype(BF))
        pv = jnp.dot(p, jnp.concatenate([vj, ones_col], axis=1), preferred_element_type=F32)
        acc_ref[sidx] = jnp.concatenate([alpha, alpha], axis=1) * acc_ref[sidx] + pv
        m_ref[sidx] = m_new

    def run(items, masked):
        ss = [scores(hh, j) for (_, hh, j) in items]
        for (sidx, hh, j), s in zip(items, ss):
            update(sidx, hh, j, s, masked)

    def body(j, carry):
        run([(hh, hh, j) for hh in range(ATT_HG)], False)
        return carry

    nfull = qi * (ATT_TQ // ATT_TK)
    lax.fori_loop(0, nfull, body, 0)
    for r in range(ATT_TQ // ATT_TK):
        run([(hh, hh, nfull + r) for hh in range(ATT_HG)], True)

    for hh in range(ATT_HG):
        acc = acc_ref[hh]
        l = jnp.sum(acc[:, V_DIM:], axis=-1, keepdims=True)
        g = gate_ref[:, V_DIM * hh: V_DIM * (hh + 1)].astype(F32)
        o_ref[:, V_DIM * hh: V_DIM * (hh + 1)] = (
            acc[:, :V_DIM] * (1.0 / l) * (g * jax.nn.sigmoid(g))).astype(BF)


def _mla_attn(qp, kp, vp, p, batch, seq):
    n = qp.shape[0]
    nq = seq // ATT_TQ
    hg = ATT_HG
    return pl.pallas_call(
        _attn_kernel,
        grid=(batch, MLA_HEADS // hg, nq),
        in_specs=[
            pl.BlockSpec((ATT_TQ, hg * HEAD_SLAB), lambda b, h, i: (b * nq + i, h)),
            pl.BlockSpec((seq, hg * HEAD_SLAB), lambda b, h, i: (b, h), pipeline_mode=pl.Buffered(1)),
            pl.BlockSpec((seq, hg * V_DIM), lambda b, h, i: (b, h)),
            pl.BlockSpec((ATT_TQ, hg * V_DIM), lambda b, h, i: (b * nq + i, P_MG // (hg * V_DIM) + h)),
        ],
        out_specs=pl.BlockSpec((ATT_TQ, hg * V_DIM), lambda b, h, i: (b * nq + i, h)),
        out_shape=jax.ShapeDtypeStruct((n, MLA_HEADS * V_DIM), BF),
        scratch_shapes=[
            pltpu.VMEM((ATT_NS, ATT_TQ, 128), F32),
            pltpu.VMEM((ATT_NS, ATT_TQ, 2 * V_DIM), F32),
        ],
        compiler_params=_cparams(("parallel", "parallel", "arbitrary")),
        name="mla_attn",
    )(qp, kp, vp, p)


GDN_T = 256
GDN_NB = GDN_T // 128
GDN_LOG2_T = GDN_T.bit_length() - 1


def _split3(x):
    a = x.astype(BF)
    r1 = x - a.astype(F32)
    b = r1.astype(BF)
    c = (r1 - b.astype(F32)).astype(BF)
    return a, b, c


def _dot_nt(a, b):
    return lax.dot_general(a, b, (((1,), (1,)), ((), ())), preferred_element_type=F32)


def _dot_tn(a, b):
    return lax.dot_general(a, b, (((0,), (0,)), ((), ())), preferred_element_type=F32)


def _gdn_kernel(q_ref, k_ref, v_ref, hq_ref, hk_ref, hv_ref, g_ref, gate_ref,
                cw_ref, arow_ref, dtrow_ref, og_ref, o_ref,
                state_ref):
    t = pl.program_id(1)
    T = GDN_T

    @pl.when(t == 0)
    def _():
        state_ref[...] = jnp.zeros(state_ref.shape, F32)

    first = t == 0

    sr = lax.broadcasted_iota(jnp.int32, ((CONV_W - 1) * 128, 128), 0)
    sc = lax.broadcasted_iota(jnp.int32, ((CONV_W - 1) * 128, 128), 1)
    shift_all = jnp.where(sc == jnp.bitwise_and(sr, 127) - (CONV_W - 1) + jnp.right_shift(sr, 7),
                          1.0, 0.0).astype(BF)
    row8 = lax.broadcasted_iota(jnp.int32, (8, 1), 0)

    def conv_silu(x_ref, h_ref, col0):
        w = [cw_ref[j:j + 1, col0:col0 + GDN_W] for j in range(CONV_W)]
        hz = jnp.where(first, 0.0, h_ref[...].astype(F32))
        ys = []
        for blk in range(GDN_NB):
            xb = x_ref[128 * blk:128 * (blk + 1), :]
            xf = xb.astype(F32)
            xs = jnp.dot(shift_all, xb, preferred_element_type=F32)
            y = xf * w[CONV_W - 1]
            corr = jnp.zeros((8, GDN_W), F32)
            for j in range(CONV_W - 1):
                y = y + xs[j * 128:(j + 1) * 128] * w[j]
                sh = CONV_W - 1 - j
                corr = corr + jnp.where(row8 < sh, pltpu.roll(hz, sh, axis=0), 0.0) * w[j]
            ys += [y[:8] + corr, y[8:]]
            hz = xf[120:]
        y = jnp.concatenate(ys, axis=0)
        return y * jax.nn.sigmoid(y)

    qc = conv_silu(q_ref, hq_ref, 0)
    kc = conv_silu(k_ref, hk_ref, GDN_W)
    vc = conv_silu(v_ref, hv_ref, 2 * GDN_W)

    gin = g_ref[...]
    gx = gin + dtrow_ref[...]
    softplus = jnp.maximum(gx, 0.0) + jnp.log1p(jnp.exp(-jnp.abs(gx)))
    gdec = -jnp.exp(arow_ref[...]) * softplus
    beta = jax.nn.sigmoid(gin)

    ri = lax.broadcasted_iota(jnp.int32, (T, T), 0)
    ci = lax.broadcasted_iota(jnp.int32, (T, T), 1)
    same_chunk = jnp.right_shift(ri, 6) == jnp.right_shift(ci, 6)
    tri = jnp.where(same_chunk, jnp.where(ci <= ri, 1.0, 0.0), 0.0).astype(BF)
    g1, g2, g3 = _split3(gdec)
    gc = (jnp.dot(tri, g1, preferred_element_type=F32)
          + jnp.dot(tri, g2, preferred_element_type=F32)
          + jnp.dot(tri, g3, preferred_element_type=F32))

    er = lax.broadcasted_iota(jnp.int32, (128, GDN_W), 0)
    ec = lax.broadcasted_iota(jnp.int32, (128, GDN_W), 1)
    e_g = jnp.where(er == jnp.right_shift(ec, 7), 1.0, 0.0).astype(BF)
    e_b = jnp.where(er == jnp.right_shift(ec, 7) + GDN_HEADS, 1.0, 0.0).astype(BF)
    c1, c2, _ = _split3(gc)
    gcb = (jnp.dot(c1, e_g, preferred_element_type=F32)
           + jnp.dot(c2, e_g, preferred_element_type=F32))
    betab = jnp.dot(beta.astype(BF), e_b, preferred_element_type=F32)
    gct = gc.T

    lane = lax.broadcasted_iota(jnp.int32, (1, 128), 1)
    lm0 = lane < CHUNK
    ii = lax.broadcasted_iota(jnp.int32, (CHUNK, 128), 0)
    jm = jnp.bitwise_and(lax.broadcasted_iota(jnp.int32, (CHUNK, 128), 1), CHUNK - 1)
    low = ii >= jm
    strict = ii > jm
    eye_p = jnp.where(ii == jm, 1.0, 0.0)
    heads = range(GDN_HEADS)
    items = [(blk, h) for blk in range(GDN_NB) for h in heads]
    nit = range(len(items))

    def bd(xp):
        return jnp.concatenate([jnp.where(lm0, xp, 0.0), jnp.where(lm0, 0.0, xp)], axis=0).astype(BF)

    def mm(a, b):
        return jnp.dot(a, b, preferred_element_type=F32)

    qd, rhs, kdt, egl, kq, dec = [], [], [], [], [], []
    for blk, h in items:
        hs = slice(GDN_DIM * h, GDN_DIM * (h + 1))
        bs = slice(128 * blk, 128 * (blk + 1))
        qh = qc[bs, hs]
        kh = kc[bs, hs]
        qss = jnp.broadcast_to(jnp.sum(qh * qh, axis=-1, keepdims=True), qh.shape)
        kss = jnp.broadcast_to(jnp.sum(kh * kh, axis=-1, keepdims=True), kh.shape)
        qh = qh * (lax.rsqrt(qss + EPS) * (GDN_DIM ** -0.5))
        kh = kh * lax.rsqrt(kss + EPS)
        bh = betab[bs, hs]
        gh = gcb[bs, hs]
        egh = jnp.exp(gh)
        kbh = kh * bh
        kt = kh.T
        rowp = gct[h:h + 1, bs]
        lhs = jnp.concatenate(
            [jnp.concatenate([kbh[:CHUNK], kbh[CHUNK:]], axis=1),
             jnp.concatenate([qh[:CHUNK], qh[CHUNK:]], axis=1)], axis=0).astype(BF)
        bdt = jnp.concatenate([jnp.where(lm0, kt, 0.0), jnp.where(lm0, 0.0, kt)], axis=0).astype(BF)
        kq.append(mm(lhs, bdt))
        colp = jnp.where(lm0, gh[:CHUNK], gh[CHUNK:])
        dec.append(jnp.where(low, jnp.exp(jnp.where(low, colp - rowp, 0.0)), 0.0))
        rhs.append(jnp.concatenate([vc[bs, hs] * bh, kbh * egh], axis=1).astype(BF))
        qd.append((qh * egh).astype(BF))
        gl0 = gh[CHUNK - 1:CHUNK, :]
        gl1 = gh[2 * CHUNK - 1:2 * CHUNK, :]
        rf0 = jnp.where(lm0, jnp.exp(jnp.where(lm0, gl0 - rowp, 0.0)), 0.0)
        rf1 = jnp.where(lm0, 0.0, jnp.exp(jnp.where(lm0, 0.0, gl1 - rowp)))
        kdt.append(((kt * rf0).astype(BF), (kt * rf1).astype(BF)))
        egl.append((jnp.exp(gl0), jnp.exp(gl1)))

    mneg = [jnp.where(strict, -(kq[i][:CHUNK] * dec[i]), 0.0) for i in nit]
    attn = [kq[i][CHUNK:] * dec[i] for i in nit]
    pinv = [eye_p + mneg[i] for i in nit]
    mp = [mm(mneg[i].astype(BF), bd(mneg[i])) for i in nit]
    for _ in range(4):
        r = [mm(jnp.concatenate([pinv[i], mp[i]], axis=0).astype(BF), bd(mp[i])) for i in nit]
        pinv = [pinv[i] + r[i][:CHUNK] for i in nit]
        mp = [r[i][CHUNK:] for i in nit]
    pinv = [pinv[i] + mm(pinv[i].astype(BF), bd(mp[i])) for i in nit]
    sol = [mm(bd(pinv[i]), rhs[i]) for i in nit]

    st = [state_ref[h] for h in heads]
    o_parts = [[] for _ in heads]
    zeros_c = jnp.zeros((CHUNK, GDN_DIM), BF)
    for blk in range(GDN_NB):
        for c in range(2):
            rs = slice(CHUNK * c, CHUNK * (c + 1))
            r1 = [mm(jnp.concatenate([sol[blk * GDN_HEADS + h][rs, GDN_DIM:].astype(BF),
                                      qd[blk * GDN_HEADS + h][rs]], axis=0), st[h].astype(BF))
                  for h in heads]
            for h in heads:
                i = blk * GDN_HEADS + h
                vn = (sol[i][rs, :GDN_DIM] - r1[h][:CHUNK]).astype(BF)
                vfull = jnp.concatenate([vn, zeros_c] if c == 0 else [zeros_c, vn], axis=0)
                am = jnp.where(lm0, attn[i], 0.0) if c == 0 else jnp.where(lm0, 0.0, attn[i])
                r2 = mm(jnp.concatenate([am.astype(BF), kdt[i][c]], axis=0), vfull)
                o_parts[h].append(r1[h][CHUNK:] + r2[:CHUNK])
                st[h] = st[h] * egl[i][c] + r2[CHUNK:]
    for h in heads:
        hs = slice(GDN_DIM * h, GDN_DIM * (h + 1))
        state_ref[h] = st[h]
        o_h = _rms(jnp.concatenate(o_parts[h], axis=0), og_ref[...])
        gt = gate_ref[:, hs].astype(F32)
        o_ref[:, hs] = (o_h * (gt * jax.nn.sigmoid(gt))).astype(BF)


def _gdn(p, g, cw, arow, dtrow, og, batch, seq):
    n = p.shape[0]
    T = GDN_T
    nt = seq // T
    tok = lambda col: pl.BlockSpec((T, GDN_W), lambda b, t: (b * nt + t, col // GDN_W))
    halo = lambda col: pl.BlockSpec(
        (8, GDN_W), lambda b, t: (jnp.maximum((b * nt + t) * (T // 8) - 1, 0), col // GDN_W))
    full = lambda r, c: pl.BlockSpec((r, c), lambda b, t: (0, 0))
    return pl.pallas_call(
        _gdn_kernel,
        grid=(batch, nt),
        in_specs=[
            tok(P_GQ), tok(P_GK), tok(P_GV), halo(P_GQ), halo(P_GK), halo(P_GV),
            pl.BlockSpec((T, 128), lambda b, t: (b * nt + t, 0)),
            tok(P_GG),
            full(CONV_W, 3 * GDN_W), full(1, 128), full(1, 128), full(1, GDN_DIM),
        ],
        out_specs=pl.BlockSpec((T, GDN_W), lambda b, t: (b * nt + t, 0)),
        out_shape=jax.ShapeDtypeStruct((n, GDN_W), BF),
        scratch_shapes=[
            pltpu.VMEM((GDN_HEADS, GDN_DIM, GDN_DIM), F32),
        ],
        compiler_params=_cparams(("parallel", "arbitrary")),
        name="gdn",
    )(p, p, p, p, p, p, g, p, cw, arow, dtrow, og)


OUT_TM, OUT_TN = 512, 512


def _outproj_kernel(a_ref, b_ref, wa_ref, wb_ref, x_ref, o_ref):
    a = a_ref[...]
    b = b_ref[...]
    for c in range(D_MODEL // OUT_TN):
        cs = slice(OUT_TN * c, OUT_TN * (c + 1))
        acc = jnp.dot(a, wa_ref[:, cs], preferred_element_type=F32)
        acc = acc + jnp.dot(b, wb_ref[:, cs], preferred_element_type=F32)
        o_ref[:, cs] = x_ref[:, cs] + acc


def _out_proj(a, b, wa, wb, x2):
    n = x2.shape[0]
    half = a.shape[1]
    return pl.pallas_call(
        _outproj_kernel,
        grid=(n // OUT_TM,),
        in_specs=[
            pl.BlockSpec((OUT_TM, half), lambda i: (i, 0)),
            pl.BlockSpec((OUT_TM, half), lambda i: (i, 0)),
            pl.BlockSpec((half, D_MODEL), lambda i: (0, 0)),
            pl.BlockSpec((half, D_MODEL), lambda i: (0, 0)),
            pl.BlockSpec((OUT_TM, D_MODEL), lambda i: (i, 0)),
        ],
        out_specs=pl.BlockSpec((OUT_TM, D_MODEL), lambda i: (i, 0)),
        out_shape=jax.ShapeDtypeStruct((n, D_MODEL), F32),
        compiler_params=_cparams(("parallel",)),
        name="out_proj",
    )(a, b, wa, wb, x2)


def _rot_cols(w):
    return jnp.concatenate([-w[..., ROPE // 2:], w[..., :ROPE // 2]], axis=-1)


def _swap_halves(g):
    return jnp.concatenate([g[..., ROPE // 2:], g[..., :ROPE // 2]], axis=-1)


def _prep_w_in(w):
    wb = w.astype(BF)
    cq, ckv, kr, mg, gq, gk, gv, ga, gb, gg = jnp.split(
        wb, [512, 768, 832, 1856, 2880, 3904, 4928, 4936, 4944], axis=1)
    pad = jnp.zeros((w.shape[0], P_COLS - P_G - 16), BF)
    return jnp.concatenate([gq, gk, gv, gg, mg, cq, ckv, kr, _rot_cols(kr), ga, gb, pad], axis=1)


def _prep_w_uq(w):
    w3 = w.reshape(Q_LORA, MLA_HEADS, QK_DIM)
    rope = w3[..., NOPE:]
    return jnp.concatenate([w3, _rot_cols(rope)], axis=-1).reshape(Q_LORA, MLA_HEADS * HEAD_SLAB).astype(BF)


def _prep_qk_gain(g):
    rope = g[NOPE:]
    return jnp.concatenate([g, _swap_halves(rope)])[None, :]


def kernel(x, positions, norm_gain, w_in, mla_q_a_gain, mla_kv_a_gain, w_uq, w_ukv,
           mla_q_norm_gain, mla_k_norm_gain, gdn_conv_w, gdn_a_log, gdn_dt_bias,
           gdn_out_norm_gain, w_out):
    batch, seq, _ = x.shape
    n = batch * seq
    x2 = x.reshape(n, D_MODEL)
    pos = positions.reshape(n, 1)
    half = ROPE // 2
    inv_freq = jnp.power(ROPE_THETA, -jnp.arange(half, dtype=F32) / half)
    invf = jnp.tile(inv_freq, 4)[None, :]

    h = x2
    for layer in range(w_in.shape[0]):
        p, g, cs = _in_proj(h, norm_gain[layer][None, :], _prep_w_in(w_in[layer]), pos, invf)
        qscale = QK_DIM ** -0.5 * LOG2E
        qp, kp, vp = _mla_prep(
            p, cs, mla_q_a_gain[layer][None, :], mla_kv_a_gain[layer][None, :],
            _prep_w_uq(w_uq[layer]), w_ukv[layer].astype(BF),
            _prep_qk_gain(mla_q_norm_gain[layer]) * qscale, _prep_qk_gain(mla_k_norm_gain[layer]))
        o_mla = _mla_attn(qp, kp, vp, p, batch, seq)
        zpad = jnp.zeros((128 - GDN_HEADS,), F32)
        arow = jnp.concatenate([gdn_a_log[layer].astype(F32), zpad])[None, :]
        dtrow = jnp.concatenate([gdn_dt_bias[layer].astype(F32), zpad])[None, :]
        o_gdn = _gdn(p, g, gdn_conv_w[layer], arow, dtrow, gdn_out_norm_gain[layer][None, :], batch, seq)
        wo = w_out[layer].astype(BF)
        h = _out_proj(o_mla, o_gdn, wo[:MLA_HEADS * V_DIM], wo[MLA_HEADS * V_DIM:], h)
    return h.reshape(batch, seq, D_MODEL)
```

```python
import functools
import math

import jax
import jax.numpy as jnp
from jax import lax
from jax.experimental import pallas as pl
from jax.experimental.pallas import tpu as pltpu

BF = jnp.bfloat16
F32 = jnp.float32

D_MODEL = 2048
MLA_HEADS = 8
NOPE = 128
ROPE = 64
QK_DIM = NOPE + ROPE
V_DIM = 128
Q_LORA = 512
KV_LORA = 256
GDN_HEADS = 8
GDN_DIM = 128
GDN_W = GDN_HEADS * GDN_DIM
CONV_W = 4
CHUNK = 64
ROPE_THETA = 10000.0
EPS = 1e-6
LOG2E = math.log2(math.e)

P_GQ, P_GK, P_GV, P_GG, P_MG = 0, 1024, 2048, 3072, 4096
P_CQ, P_CKV, P_KR, P_G = 5120, 5632, 5888, 6016
P_COLS = 6144

VMEM_LIMIT = 56 * 1024 * 1024


def _cparams(sem):
    return pltpu.CompilerParams(dimension_semantics=sem, vmem_limit_bytes=VMEM_LIMIT)


IN_TM, IN_TN = 1024, 1536
IN_NJ = P_COLS // IN_TN
IN_ROPE_ROWS = IN_TM // IN_NJ


def _inproj_kernel(x_ref, gain_ref, w_ref, pos_ref, invf_ref, p_ref, g_ref, cs_ref, xn_ref):
    j = pl.program_id(1)

    @pl.when(j == 0)
    def _():
        xf = x_ref[...]
        ms = jnp.mean(xf * xf, axis=-1, keepdims=True)
        xn_ref[...] = (xf * lax.rsqrt(ms + EPS) * gain_ref[...]).astype(BF)

    acc = jnp.dot(xn_ref[...], w_ref[...], preferred_element_type=F32)
    p_ref[...] = acc.astype(BF)

    rows = pl.ds(pl.multiple_of(j * IN_ROPE_ROWS, IN_ROPE_ROWS), IN_ROPE_ROWS)
    lane = lax.broadcasted_iota(jnp.int32, (1, 128), 1)
    ang = pos_ref[rows, :].astype(F32) * invf_ref[...]
    cs_ref[rows, :] = jnp.cos(ang - jnp.where(lane < ROPE, 0.0, 0.5 * math.pi))

    @pl.when(j == IN_NJ - 1)
    def _():
        g_ref[...] = acc[:, IN_TN - 128:]


def _in_proj(x2, gain, w, pos, invf):
    n = x2.shape[0]
    return pl.pallas_call(
        _inproj_kernel,
        grid=(n // IN_TM, IN_NJ),
        in_specs=[
            pl.BlockSpec((IN_TM, D_MODEL), lambda i, j: (i, 0)),
            pl.BlockSpec((1, D_MODEL), lambda i, j: (0, 0)),
            pl.BlockSpec((D_MODEL, IN_TN), lambda i, j: (0, j)),
            pl.BlockSpec((IN_TM, 1), lambda i, j: (i, 0)),
            pl.BlockSpec((1, 128), lambda i, j: (0, 0)),
        ],
        out_specs=[
            pl.BlockSpec((IN_TM, IN_TN), lambda i, j: (i, j)),
            pl.BlockSpec((IN_TM, 128), lambda i, j: (i, 0)),
            pl.BlockSpec((IN_TM, 128), lambda i, j: (i, 0)),
        ],
        out_shape=[
            jax.ShapeDtypeStruct((n, P_COLS), BF),
            jax.ShapeDtypeStruct((n, 128), F32),
            jax.ShapeDtypeStruct((n, 128), F32),
        ],
        scratch_shapes=[pltpu.VMEM((IN_TM, D_MODEL), BF)],
        compiler_params=_cparams(("parallel", "arbitrary")),
        name="in_proj",
    )(x2, gain, w, pos, invf)


PREP_TM = 512
HEAD_SLAB = 256


def _rms(x, gain):
    ms = jnp.mean(x * x, axis=-1, keepdims=True)
    return x * lax.rsqrt(ms + EPS) * gain


def _mla_prep_kernel(cq_ref, ckv_ref, kr_ref, cs_ref, qag_ref, kvag_ref, wuq_ref, wukv_ref,
                     qg_ref, kg_ref, q_ref, k_ref, v_ref):
    lane = lax.broadcasted_iota(jnp.int32, (1, 128), 1)
    lo64 = lane < ROPE
    cs = cs_ref[...]

    cqn = _rms(cq_ref[...].astype(F32), qag_ref[...]).astype(BF)
    y = jnp.dot(cqn, wuq_ref[...], preferred_element_type=F32)
    qg0 = qg_ref[:, :128]
    qgcs = qg_ref[:, 128:] * cs
    for h in range(MLA_HEADS):
        y0 = y[:, HEAD_SLAB * h: HEAD_SLAB * h + 128]
        y1 = y[:, HEAD_SLAB * h + 128: HEAD_SLAB * (h + 1)]
        ss = jnp.sum(y0 * y0 + jnp.where(lo64, y1 * y1, 0.0), axis=-1, keepdims=True)
        r = lax.rsqrt(ss * (1.0 / QK_DIM) + EPS)
        q_ref[:, HEAD_SLAB * h: HEAD_SLAB * h + 128] = (y0 * r * qg0).astype(BF)
        q_ref[:, HEAD_SLAB * h + 128: HEAD_SLAB * (h + 1)] = (y1 * r * qgcs).astype(BF)

    ckvn = _rms(ckv_ref[...].astype(F32), kvag_ref[...]).astype(BF)
    kv = jnp.dot(ckvn, wukv_ref[...], preferred_element_type=F32)
    kr = kr_ref[...].astype(F32)
    ssr = jnp.sum(jnp.where(lo64, kr * kr, 0.0), axis=-1, keepdims=True)
    kg0 = kg_ref[:, :128]
    kg1 = kg_ref[:, 128:]
    zk0 = kr * kg1 * cs
    zks = zk0 + pltpu.roll(zk0, ROPE, axis=1)
    for h in range(MLA_HEADS):
        kn = kv[:, HEAD_SLAB * h: HEAD_SLAB * h + 128]
        ss = jnp.sum(kn * kn, axis=-1, keepdims=True) + ssr
        r = lax.rsqrt(ss * (1.0 / QK_DIM) + EPS)
        k_ref[:, HEAD_SLAB * h: HEAD_SLAB * h + 128] = (kn * r * kg0).astype(BF)
        k_ref[:, HEAD_SLAB * h + 128: HEAD_SLAB * (h + 1)] = (zks * r).astype(BF)
        v_ref[:, V_DIM * h: V_DIM * (h + 1)] = kv[:, HEAD_SLAB * h + 128: HEAD_SLAB * (h + 1)].astype(BF)


def _mla_prep(p, cs, qag, kvag, wuq, wukv, qg, kg):
    n = p.shape[0]
    tm = PREP_TM
    full = lambda r, c: pl.BlockSpec((r, c), lambda i: (0, 0))
    return pl.pallas_call(
        _mla_prep_kernel,
        grid=(n // tm,),
        in_specs=[
            pl.BlockSpec((tm, Q_LORA), lambda i: (i, P_CQ // Q_LORA)),
            pl.BlockSpec((tm, KV_LORA), lambda i: (i, P_CKV // KV_LORA)),
            pl.BlockSpec((tm, 128), lambda i: (i, P_KR // 128)),
            pl.BlockSpec((tm, 128), lambda i: (i, 0)),
            full(1, Q_LORA), full(1, KV_LORA),
            full(Q_LORA, MLA_HEADS * HEAD_SLAB), full(KV_LORA, MLA_HEADS * HEAD_SLAB),
            full(1, HEAD_SLAB), full(1, HEAD_SLAB),
        ],
        out_specs=[
            pl.BlockSpec((tm, MLA_HEADS * HEAD_SLAB), lambda i: (i, 0)),
            pl.BlockSpec((tm, MLA_HEADS * HEAD_SLAB), lambda i: (i, 0)),
            pl.BlockSpec((tm, MLA_HEADS * V_DIM), lambda i: (i, 0)),
        ],
        out_shape=[
            jax.ShapeDtypeStruct((n, MLA_HEADS * HEAD_SLAB), BF),
            jax.ShapeDtypeStruct((n, MLA_HEADS * HEAD_SLAB), BF),
            jax.ShapeDtypeStruct((n, MLA_HEADS * V_DIM), BF),
        ],
        compiler_params=_cparams(("parallel",)),
        name="mla_prep",
    )(p, p, p, cs, qag, kvag, wuq, wukv, qg, kg)


ATT_TQ = 512
ATT_TKW = 1024
ATT_HG = 4
ATT_NS = ATT_HG
NEG = -1e30


def _attn_kernel(q_ref, k_ref, v_ref, gate_ref, o_ref, m_ref, acc_ref):
    qi = pl.program_id(2)
    m_ref[...] = jnp.full(m_ref.shape, NEG, F32)
    acc_ref[...] = jnp.zeros(acc_ref.shape, F32)
    ones_col = {
        tk: jnp.where(lax.broadcasted_iota(jnp.int32, (tk, 128), 1) == 0, 1.0, 0.0).astype(BF)
        for tk in (ATT_TQ, ATT_TKW)}

    def scores(hh, off, tk):
        q = q_ref[:, HEAD_SLAB * hh: HEAD_SLAB * (hh + 1)]
        kj = k_ref[pl.ds(off, tk), HEAD_SLAB * hh: HEAD_SLAB * (hh + 1)]
        return lax.dot_general(q, kj, (((1,), (1,)), ((), ())), preferred_element_type=F32)

    def update(hh, off, tk, s, masked):
        vj = v_ref[pl.ds(off, tk), V_DIM * hh: V_DIM * (hh + 1)]
        if masked:
            row = lax.broadcasted_iota(jnp.int32, s.shape, 0) + qi * ATT_TQ
            col = lax.broadcasted_iota(jnp.int32, s.shape, 1) + off
            s = jnp.where(col <= row, s, NEG)
        m_prev = m_ref[hh]
        m_new = jnp.maximum(m_prev, jnp.max(s, axis=-1, keepdims=True))
        alpha = jnp.exp2(m_prev - m_new)
        p = jnp.exp2((s - jnp.concatenate([m_new] * (tk // 128), axis=1)).astype(BF))
        pv = jnp.dot(p, jnp.concatenate([vj, ones_col[tk]], axis=1), preferred_element_type=F32)
        acc_ref[hh] = jnp.concatenate([alpha, alpha], axis=1) * acc_ref[hh] + pv
        m_ref[hh] = m_new

    def run(off, tk, masked):
        ss = [scores(hh, off, tk) for hh in range(ATT_HG)]
        for hh, s in enumerate(ss):
            update(hh, off, tk, s, masked)

    def body(t, carry):
        run(pl.multiple_of(t * ATT_TKW, ATT_TKW), ATT_TKW, False)
        return carry

    assert ATT_TKW == 2 * ATT_TQ
    lax.fori_loop(0, lax.shift_right_logical(qi, 1), body, 0)

    @pl.when(jnp.bitwise_and(qi, 1) == 1)
    def _():
        run(pl.multiple_of((qi - 1) * ATT_TQ, ATT_TQ), ATT_TQ, False)

    run(pl.multiple_of(qi * ATT_TQ, ATT_TQ), ATT_TQ, True)

    for hh in range(ATT_HG):
        acc = acc_ref[hh]
        l = jnp.sum(acc[:, V_DIM:], axis=-1, keepdims=True)
        g = gate_ref[:, V_DIM * hh: V_DIM * (hh + 1)].astype(F32)
        o_ref[:, V_DIM * hh: V_DIM * (hh + 1)] = (
            acc[:, :V_DIM] * (1.0 / l) * (g * jax.nn.sigmoid(g))).astype(BF)


def _mla_attn(qp, kp, vp, p, batch, seq):
    n = qp.shape[0]
    nq = seq // ATT_TQ
    hg = ATT_HG
    return pl.pallas_call(
        _attn_kernel,
        grid=(batch, MLA_HEADS // hg, nq),
        in_specs=[
            pl.BlockSpec((ATT_TQ, hg * HEAD_SLAB), lambda b, h, i: (b * nq + i, h)),
            pl.BlockSpec((seq, hg * HEAD_SLAB), lambda b, h, i: (b, h), pipeline_mode=pl.Buffered(1)),
            pl.BlockSpec((seq, hg * V_DIM), lambda b, h, i: (b, h)),
            pl.BlockSpec((ATT_TQ, hg * V_DIM), lambda b, h, i: (b * nq + i, P_MG // (hg * V_DIM) + h)),
        ],
        out_specs=pl.BlockSpec((ATT_TQ, hg * V_DIM), lambda b, h, i: (b * nq + i, h)),
        out_shape=jax.ShapeDtypeStruct((n, MLA_HEADS * V_DIM), BF),
        scratch_shapes=[
            pltpu.VMEM((ATT_NS, ATT_TQ, 128), F32),
            pltpu.VMEM((ATT_NS, ATT_TQ, 2 * V_DIM), F32),
        ],
        compiler_params=_cparams(("parallel", "parallel", "arbitrary")),
        name="mla_attn",
    )(qp, kp, vp, p)


GDN_T = 256
GDN_NB = GDN_T // 128
GDN_LOG2_T = GDN_T.bit_length() - 1


def _split3(x):
    a = x.astype(BF)
    r1 = x - a.astype(F32)
    b = r1.astype(BF)
    c = (r1 - b.astype(F32)).astype(BF)
    return a, b, c


def _dot_nt(a, b):
    return lax.dot_general(a, b, (((1,), (1,)), ((), ())), preferred_element_type=F32)


def _dot_tn(a, b):
    return lax.dot_general(a, b, (((0,), (0,)), ((), ())), preferred_element_type=F32)


def _gdn_kernel(q_ref, k_ref, v_ref, hq_ref, hk_ref, hv_ref, g_ref, gate_ref,
                cw_ref, arow_ref, dtrow_ref, og_ref, o_ref,
                state_ref):
    t = pl.program_id(1)
    T = GDN_T

    @pl.when(t == 0)
    def _():
        state_ref[...] = jnp.zeros(state_ref.shape, F32)

    first = t == 0

    sr = lax.broadcasted_iota(jnp.int32, (128, CONV_W * 128), 0)
    sc = lax.broadcasted_iota(jnp.int32, (128, CONV_W * 128), 1)
    shift_cat = jnp.where(jnp.bitwise_and(sc, 127) == sr - (CONV_W - 1) + jnp.right_shift(sc, 7),
                          1.0, 0.0).astype(BF)
    row8 = lax.broadcasted_iota(jnp.int32, (8, 1), 0)

    def conv_silu(x_ref, h_ref, col0):
        w = [cw_ref[j:j + 1, col0:col0 + GDN_W] for j in range(CONV_W)]
        wb = [wj.astype(BF) for wj in w]
        hz = jnp.where(first, 0.0, h_ref[...].astype(F32))
        ys = []
        for blk in range(GDN_NB):
            xb = x_ref[128 * blk:128 * (blk + 1), :]
            taps = jnp.concatenate([xb * wb[j] for j in range(CONV_W)], axis=0)
            y = jnp.dot(shift_cat, taps, preferred_element_type=F32)
            corr = jnp.zeros((8, GDN_W), F32)
            for j in range(CONV_W - 1):
                sh = CONV_W - 1 - j
                corr = corr + jnp.where(row8 < sh, pltpu.roll(hz, sh, axis=0), 0.0) * w[j]
            ys += [y[:8] + corr, y[8:]]
            hz = xb[112:].astype(F32)[8:]
        y = jnp.concatenate(ys, axis=0)
        return y * jax.nn.sigmoid(y)

    qc = conv_silu(q_ref, hq_ref, 0)
    kc = conv_silu(k_ref, hk_ref, GDN_W)
    vc = conv_silu(v_ref, hv_ref, 2 * GDN_W)

    gin = g_ref[...]
    gx = gin + dtrow_ref[...]
    softplus = jnp.maximum(gx, 0.0) + jnp.log1p(jnp.exp(-jnp.abs(gx)))
    gdec = -jnp.exp(arow_ref[...]) * softplus
    beta = jax.nn.sigmoid(gin)

    ri = lax.broadcasted_iota(jnp.int32, (T, T), 0)
    ci = lax.broadcasted_iota(jnp.int32, (T, T), 1)
    same_chunk = jnp.right_shift(ri, 6) == jnp.right_shift(ci, 6)
    tri = jnp.where(same_chunk, jnp.where(ci <= ri, 1.0, 0.0), 0.0).astype(BF)
    g1, g2, g3 = _split3(gdec)
    gc = (jnp.dot(tri, g1, preferred_element_type=F32)
          + jnp.dot(tri, g2, preferred_element_type=F32)
          + jnp.dot(tri, g3, preferred_element_type=F32))

    er = lax.broadcasted_iota(jnp.int32, (128, GDN_W), 0)
    ec = lax.broadcasted_iota(jnp.int32, (128, GDN_W), 1)
    e_g = jnp.where(er == jnp.right_shift(ec, 7), 1.0, 0.0).astype(BF)
    e_b = jnp.where(er == jnp.right_shift(ec, 7) + GDN_HEADS, 1.0, 0.0).astype(BF)
    c1, c2, _ = _split3(gc)
    gcb = (jnp.dot(c1, e_g, preferred_element_type=F32)
           + jnp.dot(c2, e_g, preferred_element_type=F32))
    betab = jnp.dot(beta.astype(BF), e_b, preferred_element_type=F32)
    gct = gc.T

    lane = lax.broadcasted_iota(jnp.int32, (1, 128), 1)
    lm0 = lane < CHUNK
    ii = lax.broadcasted_iota(jnp.int32, (CHUNK, 128), 0)
    jm = jnp.bitwise_and(lax.broadcasted_iota(jnp.int32, (CHUNK, 128), 1), CHUNK - 1)
    low = ii >= jm
    strict = ii > jm
    eye_p = jnp.where(ii == jm, 1.0, 0.0)
    heads = range(GDN_HEADS)
    items = [(blk, h) for blk in range(GDN_NB) for h in heads]
    nit = range(len(items))

    def bd(xp):
        return jnp.concatenate([jnp.where(lm0, xp, 0.0), jnp.where(lm0, 0.0, xp)], axis=0).astype(BF)

    def mm(a, b):
        return jnp.dot(a, b, preferred_element_type=F32)

    qd, rhs, kdt, egl, kq, dec = [], [], [], [], [], []
    for blk, h in items:
        hs = slice(GDN_DIM * h, GDN_DIM * (h + 1))
        bs = slice(128 * blk, 128 * (blk + 1))
        qh = qc[bs, hs]
        kh = kc[bs, hs]
        qss = jnp.broadcast_to(jnp.sum(qh * qh, axis=-1, keepdims=True), qh.shape)
        kss = jnp.broadcast_to(jnp.sum(kh * kh, axis=-1, keepdims=True), kh.shape)
        qh = qh * (lax.rsqrt(qss + EPS) * (GDN_DIM ** -0.5))
        kh = kh * lax.rsqrt(kss + EPS)
        bh = betab[bs, hs]
        gh = gcb[bs, hs]
        egh = jnp.exp(gh)
        kbh = kh * bh
        kt = kh.T
        rowp = gct[h:h + 1, bs]
        lhs = jnp.concatenate(
            [jnp.concatenate([kbh[:CHUNK], kbh[CHUNK:]], axis=1),
             jnp.concatenate([qh[:CHUNK], qh[CHUNK:]], axis=1)], axis=0).astype(BF)
        bdt = jnp.concatenate([jnp.where(lm0, kt, 0.0), jnp.where(lm0, 0.0, kt)], axis=0).astype(BF)
        kq.append(mm(lhs, bdt))
        colp = jnp.where(lm0, gh[:CHUNK], gh[CHUNK:])
        dec.append(jnp.where(low, jnp.exp(jnp.where(low, colp - rowp, 0.0)), 0.0))
        rhs.append(jnp.concatenate([vc[bs, hs] * bh, kbh * egh], axis=1).astype(BF))
        qd.append((qh * egh).astype(BF))
        gl0 = gh[CHUNK - 1:CHUNK, :]
        gl1 = gh[2 * CHUNK - 1:2 * CHUNK, :]
        rf0 = jnp.where(lm0, jnp.exp(jnp.where(lm0, gl0 - rowp, 0.0)), 0.0)
        rf1 = jnp.where(lm0, 0.0, jnp.exp(jnp.where(lm0, 0.0, gl1 - rowp)))
        kdt.append(((kt * rf0).astype(BF), (kt * rf1).astype(BF)))
        egl.append((jnp.exp(gl0), jnp.exp(gl1)))

    mneg = [jnp.where(strict, -(kq[i][:CHUNK] * dec[i]), 0.0) for i in nit]
    attn = [kq[i][CHUNK:] * dec[i] for i in nit]
    pinv = [eye_p + mneg[i] for i in nit]
    mp = [mm(mneg[i].astype(BF), bd(mneg[i])) for i in nit]
    for _ in range(4):
        r = [mm(jnp.concatenate([pinv[i], mp[i]], axis=0).astype(BF), bd(mp[i])) for i in nit]
        pinv = [pinv[i] + r[i][:CHUNK] for i in nit]
        mp = [r[i][CHUNK:] for i in nit]
    pinv = [pinv[i] + mm(pinv[i].astype(BF), bd(mp[i])) for i in nit]
    sol = [mm(bd(pinv[i]), rhs[i]) for i in nit]

    st = [state_ref[h] for h in heads]
    o_parts = [[] for _ in heads]
    zeros_c = jnp.zeros((CHUNK, GDN_DIM), BF)
    for blk in range(GDN_NB):
        for c in range(2):
            rs = slice(CHUNK * c, CHUNK * (c + 1))
            r1 = [mm(jnp.concatenate([sol[blk * GDN_HEADS + h][rs, GDN_DIM:].astype(BF),
                                      qd[blk * GDN_HEADS + h][rs]], axis=0), st[h].astype(BF))
                  for h in heads]
            for h in heads:
                i = blk * GDN_HEADS + h
                vn = (sol[i][rs, :GDN_DIM] - r1[h][:CHUNK]).astype(BF)
                vfull = jnp.concatenate([vn, zeros_c] if c == 0 else [zeros_c, vn], axis=0)
                am = jnp.where(lm0, attn[i], 0.0) if c == 0 else jnp.where(lm0, 0.0, attn[i])
                r2 = mm(jnp.concatenate([am.astype(BF), kdt[i][c]], axis=0), vfull)
                o_parts[h].append(r1[h][CHUNK:] + r2[:CHUNK])
                st[h] = st[h] * egl[i][c] + r2[CHUNK:]
    for h in heads:
        hs = slice(GDN_DIM * h, GDN_DIM * (h + 1))
        state_ref[h] = st[h]
        o_h = _rms(jnp.concatenate(o_parts[h], axis=0), og_ref[...])
        gt = gate_ref[:, hs].astype(F32)
        o_ref[:, hs] = (o_h * (gt * jax.nn.sigmoid(gt))).astype(BF)


def _gdn(p, g, cw, arow, dtrow, og, batch, seq):
    n = p.shape[0]
    T = GDN_T
    nt = seq // T
    tok = lambda col: pl.BlockSpec((T, GDN_W), lambda b, t: (b * nt + t, col // GDN_W))
    halo = lambda col: pl.BlockSpec(
        (8, GDN_W), lambda b, t: (jnp.maximum((b * nt + t) * (T // 8) - 1, 0), col // GDN_W))
    full = lambda r, c: pl.BlockSpec((r, c), lambda b, t: (0, 0))
    return pl.pallas_call(
        _gdn_kernel,
        grid=(batch, nt),
        in_specs=[
            tok(P_GQ), tok(P_GK), tok(P_GV), halo(P_GQ), halo(P_GK), halo(P_GV),
            pl.BlockSpec((T, 128), lambda b, t: (b * nt + t, 0)),
            tok(P_GG),
            full(CONV_W, 3 * GDN_W), full(1, 128), full(1, 128), full(1, GDN_DIM),
        ],
        out_specs=pl.BlockSpec((T, GDN_W), lambda b, t: (b * nt + t, 0)),
        out_shape=jax.ShapeDtypeStruct((n, GDN_W), BF),
        scratch_shapes=[
            pltpu.VMEM((GDN_HEADS, GDN_DIM, GDN_DIM), F32),
        ],
        compiler_params=_cparams(("parallel", "arbitrary")),
        name="gdn",
    )(p, p, p, p, p, p, g, p, cw, arow, dtrow, og)


OUT_TM, OUT_TN = 512, 512


def _outproj_kernel(a_ref, b_ref, wa_ref, wb_ref, x_ref, o_ref):
    a = a_ref[...]
    b = b_ref[...]
    for c in range(D_MODEL // OUT_TN):
        cs = slice(OUT_TN * c, OUT_TN * (c + 1))
        acc = jnp.dot(a, wa_ref[:, cs], preferred_element_type=F32)
        acc = acc + jnp.dot(b, wb_ref[:, cs], preferred_element_type=F32)
        o_ref[:, cs] = x_ref[:, cs] + acc


def _out_proj(a, b, wa, wb, x2):
    n = x2.shape[0]
    half = a.shape[1]
    return pl.pallas_call(
        _outproj_kernel,
        grid=(n // OUT_TM,),
        in_specs=[
            pl.BlockSpec((OUT_TM, half), lambda i: (i, 0)),
            pl.BlockSpec((OUT_TM, half), lambda i: (i, 0)),
            pl.BlockSpec((half, D_MODEL), lambda i: (0, 0)),
            pl.BlockSpec((half, D_MODEL), lambda i: (0, 0)),
            pl.BlockSpec((OUT_TM, D_MODEL), lambda i: (i, 0)),
        ],
        out_specs=pl.BlockSpec((OUT_TM, D_MODEL), lambda i: (i, 0)),
        out_shape=jax.ShapeDtypeStruct((n, D_MODEL), F32),
        compiler_params=_cparams(("parallel",)),
        name="out_proj",
    )(a, b, wa, wb, x2)


def _rot_cols(w):
    return jnp.concatenate([-w[..., ROPE // 2:], w[..., :ROPE // 2]], axis=-1)


def _swap_halves(g):
    return jnp.concatenate([g[..., ROPE // 2:], g[..., :ROPE // 2]], axis=-1)


def _prep_w_in(w):
    wb = w.astype(BF)
    cq, ckv, kr, mg, gq, gk, gv, ga, gb, gg = jnp.split(
        wb, [512, 768, 832, 1856, 2880, 3904, 4928, 4936, 4944], axis=1)
    pad = jnp.zeros((w.shape[0], P_COLS - P_G - 16), BF)
    return jnp.concatenate([gq, gk, gv, gg, mg, cq, ckv, kr, _rot_cols(kr), ga, gb, pad], axis=1)


def _prep_w_uq(w):
    w3 = w.reshape(Q_LORA, MLA_HEADS, QK_DIM)
    rope = w3[..., NOPE:]
    return jnp.concatenate([w3, _rot_cols(rope)], axis=-1).reshape(Q_LORA, MLA_HEADS * HEAD_SLAB).astype(BF)


def _prep_qk_gain(g):
    rope = g[NOPE:]
    return jnp.concatenate([g, _swap_halves(rope)])[None, :]


def kernel(x, positions, norm_gain, w_in, mla_q_a_gain, mla_kv_a_gain, w_uq, w_ukv,
           mla_q_norm_gain, mla_k_norm_gain, gdn_conv_w, gdn_a_log, gdn_dt_bias,
           gdn_out_norm_gain, w_out):
    batch, seq, _ = x.shape
    n = batch * seq
    x2 = x.reshape(n, D_MODEL)
    pos = positions.reshape(n, 1)
    half = ROPE // 2
    inv_freq = jnp.power(ROPE_THETA, -jnp.arange(half, dtype=F32) / half)
    invf = jnp.tile(inv_freq, 4)[None, :]

    h = x2
    for layer in range(w_in.shape[0]):
        p, g, cs = _in_proj(h, norm_gain[layer][None, :], _prep_w_in(w_in[layer]), pos, invf)
        qscale = QK_DIM ** -0.5 * LOG2E
        qp, kp, vp = _mla_prep(
            p, cs, mla_q_a_gain[layer][None, :], mla_kv_a_gain[layer][None, :],
            _prep_w_uq(w_uq[layer]), w_ukv[layer].astype(BF),
            _prep_qk_gain(mla_q_norm_gain[layer]) * qscale, _prep_qk_gain(mla_k_norm_gain[layer]))
        o_mla = _mla_attn(qp, kp, vp, p, batch, seq)
        zpad = jnp.zeros((128 - GDN_HEADS,), F32)
        arow = jnp.concatenate([gdn_a_log[layer].astype(F32), zpad])[None, :]
        dtrow = jnp.concatenate([gdn_dt_bias[layer].astype(F32), zpad])[None, :]
        o_gdn = _gdn(p, g, gdn_conv_w[layer], arow, dtrow, gdn_out_norm_gain[layer][None, :], batch, seq)
        wo = w_out[layer].astype(BF)
        h = _out_proj(o_mla, o_gdn, wo[:MLA_HEADS * V_DIM], wo[MLA_HEADS * V_DIM:], h)
    return h.reshape(batch, seq, D_MODEL)
```

```python
import functools
import math

import jax
import jax.numpy as jnp
from jax import lax
from jax.experimental import pallas as pl
from jax.experimental.pallas import tpu as pltpu

BF = jnp.bfloat16
F32 = jnp.float32

D_MODEL = 2048
MLA_HEADS = 8
NOPE = 128
ROPE = 64
QK_DIM = NOPE + ROPE
V_DIM = 128
Q_LORA = 512
KV_LORA = 256
GDN_HEADS = 8
GDN_DIM = 128
GDN_W = GDN_HEADS * GDN_DIM
CONV_W = 4
CHUNK = 64
ROPE_THETA = 10000.0
EPS = 1e-6
LOG2E = math.log2(math.e)

P_GQ, P_GK, P_GV, P_GG, P_MG = 0, 1024, 2048, 3072, 4096
P_CQ, P_CKV, P_KR, P_G = 5120, 5632, 5888, 6016
P_COLS = 6144

VMEM_LIMIT = 56 * 1024 * 1024


def _cparams(sem):
    return pltpu.CompilerParams(dimension_semantics=sem, vmem_limit_bytes=VMEM_LIMIT)


IN_TM, IN_TN = 1024, 1536
IN_NJ = P_COLS // IN_TN
IN_ROPE_ROWS = IN_TM // IN_NJ


def _inproj_kernel(x_ref, gain_ref, w_ref, pos_ref, invf_ref, p_ref, g_ref, cs_ref, xn_ref):
    j = pl.program_id(1)

    @pl.when(j == 0)
    def _():
        xf = x_ref[...]
        ms = jnp.mean(xf * xf, axis=-1, keepdims=True)
        xn_ref[...] = (xf * lax.rsqrt(ms + EPS) * gain_ref[...]).astype(BF)

    acc = jnp.dot(xn_ref[...], w_ref[...], preferred_element_type=F32)
    p_ref[...] = acc.astype(BF)

    rows = pl.ds(pl.multiple_of(j * IN_ROPE_ROWS, IN_ROPE_ROWS), IN_ROPE_ROWS)
    lane = lax.broadcasted_iota(jnp.int32, (1, 128), 1)
    ang = pos_ref[rows, :].astype(F32) * invf_ref[...]
    cs_ref[rows, :] = jnp.cos(ang - jnp.where(lane < ROPE, 0.0, 0.5 * math.pi))

    @pl.when(j == IN_NJ - 1)
    def _():
        g_ref[...] = acc[:, IN_TN - 128:]


def _in_proj(x2, gain, w, pos, invf):
    n = x2.shape[0]
    return pl.pallas_call(
        _inproj_kernel,
        grid=(n // IN_TM, IN_NJ),
        in_specs=[
            pl.BlockSpec((IN_TM, D_MODEL), lambda i, j: (i, 0)),
            pl.BlockSpec((1, D_MODEL), lambda i, j: (0, 0)),
            pl.BlockSpec((D_MODEL, IN_TN), lambda i, j: (0, j)),
            pl.BlockSpec((IN_TM, 1), lambda i, j: (i, 0)),
            pl.BlockSpec((1, 128), lambda i, j: (0, 0)),
        ],
        out_specs=[
            pl.BlockSpec((IN_TM, IN_TN), lambda i, j: (i, j)),
            pl.BlockSpec((IN_TM, 128), lambda i, j: (i, 0)),
            pl.BlockSpec((IN_TM, 128), lambda i, j: (i, 0)),
        ],
        out_shape=[
            jax.ShapeDtypeStruct((n, P_COLS), BF),
            jax.ShapeDtypeStruct((n, 128), F32),
            jax.ShapeDtypeStruct((n, 128), F32),
        ],
        scratch_shapes=[pltpu.VMEM((IN_TM, D_MODEL), BF)],
        compiler_params=_cparams(("parallel", "arbitrary")),
        name="in_proj",
    )(x2, gain, w, pos, invf)


PREP_TM = 512
HEAD_SLAB = 256


def _rms(x, gain):
    ms = jnp.mean(x * x, axis=-1, keepdims=True)
    return x * lax.rsqrt(ms + EPS) * gain


def _mla_prep_kernel(cq_ref, ckv_ref, kr_ref, cs_ref, qag_ref, kvag_ref, wuq_ref, wukv_ref,
                     qg_ref, kg_ref, q_ref, k_ref, v_ref):
    lane = lax.broadcasted_iota(jnp.int32, (1, 128), 1)
    lo64 = lane < ROPE
    cs = cs_ref[...]

    cqn = _rms(cq_ref[...].astype(F32), qag_ref[...]).astype(BF)
    y = jnp.dot(cqn, wuq_ref[...], preferred_element_type=F32)
    qg0 = qg_ref[:, :128]
    qgcs = qg_ref[:, 128:] * cs
    for h in range(MLA_HEADS):
        y0 = y[:, HEAD_SLAB * h: HEAD_SLAB * h + 128]
        y1 = y[:, HEAD_SLAB * h + 128: HEAD_SLAB * (h + 1)]
        ss = jnp.sum(y0 * y0 + jnp.where(lo64, y1 * y1, 0.0), axis=-1, keepdims=True)
        r = lax.rsqrt(ss * (1.0 / QK_DIM) + EPS)
        q_ref[:, HEAD_SLAB * h: HEAD_SLAB * h + 128] = (y0 * r * qg0).astype(BF)
        q_ref[:, HEAD_SLAB * h + 128: HEAD_SLAB * (h + 1)] = (y1 * r * qgcs).astype(BF)

    ckvn = _rms(ckv_ref[...].astype(F32), kvag_ref[...]).astype(BF)
    kv = jnp.dot(ckvn, wukv_ref[...], preferred_element_type=F32)
    kr = kr_ref[...].astype(F32)
    ssr = jnp.sum(jnp.where(lo64, kr * kr, 0.0), axis=-1, keepdims=True)
    kg0 = kg_ref[:, :128]
    kg1 = kg_ref[:, 128:]
    zk0 = kr * kg1 * cs
    zks = zk0 + pltpu.roll(zk0, ROPE, axis=1)
    for h in range(MLA_HEADS):
        kn = kv[:, HEAD_SLAB * h: HEAD_SLAB * h + 128]
        ss = jnp.sum(kn * kn, axis=-1, keepdims=True) + ssr
        r = lax.rsqrt(ss * (1.0 / QK_DIM) + EPS)
        k_ref[:, HEAD_SLAB * h: HEAD_SLAB * h + 128] = (kn * r * kg0).astype(BF)
        k_ref[:, HEAD_SLAB * h + 128: HEAD_SLAB * (h + 1)] = (zks * r).astype(BF)
        v_ref[:, V_DIM * h: V_DIM * (h + 1)] = kv[:, HEAD_SLAB * h + 128: HEAD_SLAB * (h + 1)].astype(BF)


def _mla_prep(p, cs, qag, kvag, wuq, wukv, qg, kg):
    n = p.shape[0]
    tm = PREP_TM
    full = lambda r, c: pl.BlockSpec((r, c), lambda i: (0, 0))
    return pl.pallas_call(
        _mla_prep_kernel,
        grid=(n // tm,),
        in_specs=[
            pl.BlockSpec((tm, Q_LORA), lambda i: (i, P_CQ // Q_LORA)),
            pl.BlockSpec((tm, KV_LORA), lambda i: (i, P_CKV // KV_LORA)),
            pl.BlockSpec((tm, 128), lambda i: (i, P_KR // 128)),
            pl.BlockSpec((tm, 128), lambda i: (i, 0)),
            full(1, Q_LORA), full(1, KV_LORA),
            full(Q_LORA, MLA_HEADS * HEAD_SLAB), full(KV_LORA, MLA_HEADS * HEAD_SLAB),
            full(1, HEAD_SLAB), full(1, HEAD_SLAB),
        ],
        out_specs=[
            pl.BlockSpec((tm, MLA_HEADS * HEAD_SLAB), lambda i: (i, 0)),
            pl.BlockSpec((tm, MLA_HEADS * HEAD_SLAB), lambda i: (i, 0)),
            pl.BlockSpec((tm, MLA_HEADS * V_DIM), lambda i: (i, 0)),
        ],
        out_shape=[
            jax.ShapeDtypeStruct((n, MLA_HEADS * HEAD_SLAB), BF),
            jax.ShapeDtypeStruct((n, MLA_HEADS * HEAD_SLAB), BF),
            jax.ShapeDtypeStruct((n, MLA_HEADS * V_DIM), BF),
        ],
        compiler_params=_cparams(("parallel",)),
        name="mla_prep",
    )(p, p, p, cs, qag, kvag, wuq, wukv, qg, kg)


ATT_TQ = 512
ATT_TKW = 1024
ATT_HG = 4
ATT_NS = ATT_HG
NEG = -1e30


def _attn_kernel(q_ref, k_ref, v_ref, gate_ref, o_ref, m_ref, acc_ref):
    qi = pl.program_id(2)
    m_ref[...] = jnp.full(m_ref.shape, NEG, F32)
    acc_ref[...] = jnp.zeros(acc_ref.shape, F32)
    ones_col = {
        tk: jnp.where(lax.broadcasted_iota(jnp.int32, (tk, 128), 1) == 0, 1.0, 0.0).astype(BF)
        for tk in (ATT_TQ, ATT_TKW)}

    def scores(hh, off, tk):
        q = q_ref[:, HEAD_SLAB * hh: HEAD_SLAB * (hh + 1)]
        kj = k_ref[pl.ds(off, tk), HEAD_SLAB * hh: HEAD_SLAB * (hh + 1)]
        return lax.dot_general(q, kj, (((1,), (1,)), ((), ())), preferred_element_type=F32)

    def update(hh, off, tk, s, masked):
        vj = v_ref[pl.ds(off, tk), V_DIM * hh: V_DIM * (hh + 1)]
        if masked:
            row = lax.broadcasted_iota(jnp.int32, s.shape, 0) + qi * ATT_TQ
            col = lax.broadcasted_iota(jnp.int32, s.shape, 1) + off
            s = jnp.where(col <= row, s, NEG)
        m_prev = m_ref[hh]
        m_new = jnp.maximum(m_prev, jnp.max(s, axis=-1, keepdims=True))
        alpha = jnp.exp2(m_prev - m_new)
        p = jnp.exp2((s - jnp.concatenate([m_new] * (tk // 128), axis=1)).astype(BF))
        pv = jnp.dot(p, jnp.concatenate([vj, ones_col[tk]], axis=1), preferred_element_type=F32)
        acc_ref[hh] = jnp.concatenate([alpha, alpha], axis=1) * acc_ref[hh] + pv
        m_ref[hh] = m_new

    def run(off, tk, masked):
        ss = [scores(hh, off, tk) for hh in range(ATT_HG)]
        for hh, s in enumerate(ss):
            update(hh, off, tk, s, masked)

    def body(t, carry):
        run(pl.multiple_of(t * ATT_TKW, ATT_TKW), ATT_TKW, False)
        return carry

    assert ATT_TKW == 2 * ATT_TQ
    lax.fori_loop(0, lax.shift_right_logical(qi, 1), body, 0)

    @pl.when(jnp.bitwise_and(qi, 1) == 1)
    def _():
        run(pl.multiple_of((qi - 1) * ATT_TQ, ATT_TQ), ATT_TQ, False)

    run(pl.multiple_of(qi * ATT_TQ, ATT_TQ), ATT_TQ, True)

    for hh in range(ATT_HG):
        acc = acc_ref[hh]
        l = jnp.sum(acc[:, V_DIM:], axis=-1, keepdims=True)
        g = gate_ref[:, V_DIM * hh: V_DIM * (hh + 1)].astype(F32)
        o_ref[:, V_DIM * hh: V_DIM * (hh + 1)] = (
            acc[:, :V_DIM] * (1.0 / l) * (g * jax.nn.sigmoid(g))).astype(BF)


def _mla_attn(qp, kp, vp, p, batch, seq):
    n = qp.shape[0]
    nq = seq // ATT_TQ
    hg = ATT_HG
    return pl.pallas_call(
        _attn_kernel,
        grid=(batch, MLA_HEADS // hg, nq),
        in_specs=[
            pl.BlockSpec((ATT_TQ, hg * HEAD_SLAB), lambda b, h, i: (b * nq + i, h)),
            pl.BlockSpec((seq, hg * HEAD_SLAB), lambda b, h, i: (b, h), pipeline_mode=pl.Buffered(1)),
            pl.BlockSpec((seq, hg * V_DIM), lambda b, h, i: (b, h)),
            pl.BlockSpec((ATT_TQ, hg * V_DIM), lambda b, h, i: (b * nq + i, P_MG // (hg * V_DIM) + h)),
        ],
        out_specs=pl.BlockSpec((ATT_TQ, hg * V_DIM), lambda b, h, i: (b * nq + i, h)),
        out_shape=jax.ShapeDtypeStruct((n, MLA_HEADS * V_DIM), BF),
        scratch_shapes=[
            pltpu.VMEM((ATT_NS, ATT_TQ, 128), F32),
            pltpu.VMEM((ATT_NS, ATT_TQ, 2 * V_DIM), F32),
        ],
        compiler_params=_cparams(("parallel", "parallel", "arbitrary")),
        name="mla_attn",
    )(qp, kp, vp, p)


GDN_T = 512
GDN_NB = GDN_T // 128
GDN_LOG2_T = GDN_T.bit_length() - 1


def _split3(x):
    a = x.astype(BF)
    r1 = x - a.astype(F32)
    b = r1.astype(BF)
    c = (r1 - b.astype(F32)).astype(BF)
    return a, b, c


def _dot_nt(a, b):
    return lax.dot_general(a, b, (((1,), (1,)), ((), ())), preferred_element_type=F32)


def _dot_tn(a, b):
    return lax.dot_general(a, b, (((0,), (0,)), ((), ())), preferred_element_type=F32)


def _gdn_kernel(q_ref, k_ref, v_ref, hq_ref, hk_ref, hv_ref, g_ref, gate_ref,
                cw_ref, arow_ref, dtrow_ref, og_ref, o_ref,
                state_ref):
    t = pl.program_id(1)
    T = GDN_T

    @pl.when(t == 0)
    def _():
        state_ref[...] = jnp.zeros(state_ref.shape, F32)

    first = t == 0

    sr = lax.broadcasted_iota(jnp.int32, (128, CONV_W * 128), 0)
    sc = lax.broadcasted_iota(jnp.int32, (128, CONV_W * 128), 1)
    shift_cat = jnp.where(jnp.bitwise_and(sc, 127) == sr - (CONV_W - 1) + jnp.right_shift(sc, 7),
                          1.0, 0.0).astype(BF)
    row8 = lax.broadcasted_iota(jnp.int32, (8, 1), 0)

    def conv_silu(x_ref, h_ref, col0):
        w = [cw_ref[j:j + 1, col0:col0 + GDN_W] for j in range(CONV_W)]
        wb = [wj.astype(BF) for wj in w]
        hz = jnp.where(first, 0.0, h_ref[...].astype(F32))
        ys = []
        for blk in range(GDN_NB):
            xb = x_ref[128 * blk:128 * (blk + 1), :]
            taps = jnp.concatenate([xb * wb[j] for j in range(CONV_W)], axis=0)
            y = jnp.dot(shift_cat, taps, preferred_element_type=F32)
            corr = jnp.zeros((8, GDN_W), F32)
            for j in range(CONV_W - 1):
                sh = CONV_W - 1 - j
                corr = corr + jnp.where(row8 < sh, pltpu.roll(hz, sh, axis=0), 0.0) * w[j]
            ys += [y[:8] + corr, y[8:]]
            hz = xb[112:].astype(F32)[8:]
        y = jnp.concatenate(ys, axis=0)
        return y * jax.nn.sigmoid(y)

    qc = conv_silu(q_ref, hq_ref, 0)
    kc = conv_silu(k_ref, hk_ref, GDN_W)
    vc = conv_silu(v_ref, hv_ref, 2 * GDN_W)

    gin = g_ref[...]
    gx = gin + dtrow_ref[...]
    softplus = jnp.maximum(gx, 0.0) + jnp.log1p(jnp.exp(-jnp.abs(gx)))
    gdec = -jnp.exp(arow_ref[...]) * softplus
    beta = jax.nn.sigmoid(gin)

    ri = lax.broadcasted_iota(jnp.int32, (T, T), 0)
    ci = lax.broadcasted_iota(jnp.int32, (T, T), 1)
    same_chunk = jnp.right_shift(ri, 6) == jnp.right_shift(ci, 6)
    tri = jnp.where(same_chunk, jnp.where(ci <= ri, 1.0, 0.0), 0.0).astype(BF)
    g1, g2, g3 = _split3(gdec)
    gc = (jnp.dot(tri, g1, preferred_element_type=F32)
          + jnp.dot(tri, g2, preferred_element_type=F32)
          + jnp.dot(tri, g3, preferred_element_type=F32))

    er = lax.broadcasted_iota(jnp.int32, (128, GDN_W), 0)
    ec = lax.broadcasted_iota(jnp.int32, (128, GDN_W), 1)
    e_g = jnp.where(er == jnp.right_shift(ec, 7), 1.0, 0.0).astype(BF)
    e_b = jnp.where(er == jnp.right_shift(ec, 7) + GDN_HEADS, 1.0, 0.0).astype(BF)
    c1, c2, _ = _split3(gc)
    gcb = (jnp.dot(c1, e_g, preferred_element_type=F32)
           + jnp.dot(c2, e_g, preferred_element_type=F32))
    betab = jnp.dot(beta.astype(BF), e_b, preferred_element_type=F32)
    gct = gc.T

    lane = lax.broadcasted_iota(jnp.int32, (1, 128), 1)
    lm0 = lane < CHUNK
    ii = lax.broadcasted_iota(jnp.int32, (CHUNK, 128), 0)
    jm = jnp.bitwise_and(lax.broadcasted_iota(jnp.int32, (CHUNK, 128), 1), CHUNK - 1)
    low = ii >= jm
    strict = ii > jm
    eye_p = jnp.where(ii == jm, 1.0, 0.0)
    heads = range(GDN_HEADS)
    items = [(blk, h) for blk in range(GDN_NB) for h in heads]
    nit = range(len(items))

    def bd(xp):
        return jnp.concatenate([jnp.where(lm0, xp, 0.0), jnp.where(lm0, 0.0, xp)], axis=0).astype(BF)

    def mm(a, b):
        return jnp.dot(a, b, preferred_element_type=F32)

    qd, rhs, kdt, egl, kq, dec = [], [], [], [], [], []
    for blk, h in items:
        hs = slice(GDN_DIM * h, GDN_DIM * (h + 1))
        bs = slice(128 * blk, 128 * (blk + 1))
        qh = qc[bs, hs]
        kh = kc[bs, hs]
        qss = jnp.broadcast_to(jnp.sum(qh * qh, axis=-1, keepdims=True), qh.shape)
        kss = jnp.broadcast_to(jnp.sum(kh * kh, axis=-1, keepdims=True), kh.shape)
        qh = qh * (lax.rsqrt(qss + EPS) * (GDN_DIM ** -0.5))
        kh = kh * lax.rsqrt(kss + EPS)
        bh = betab[bs, hs]
        gh = gcb[bs, hs]
        egh = jnp.exp(gh)
        kbh = kh * bh
        kt = kh.T
        rowp = gct[h:h + 1, bs]
        lhs = jnp.concatenate(
            [jnp.concatenate([kbh[:CHUNK], kbh[CHUNK:]], axis=1),
             jnp.concatenate([qh[:CHUNK], qh[CHUNK:]], axis=1)], axis=0).astype(BF)
        bdt = jnp.concatenate([jnp.where(lm0, kt, 0.0), jnp.where(lm0, 0.0, kt)], axis=0).astype(BF)
        kq.append(mm(lhs, bdt))
        colp = jnp.where(lm0, gh[:CHUNK], gh[CHUNK:])
        dec.append(jnp.where(low, jnp.exp(jnp.where(low, colp - rowp, 0.0)), 0.0))
        rhs.append(jnp.concatenate([vc[bs, hs] * bh, kbh * egh], axis=1).astype(BF))
        qd.append((qh * egh).astype(BF))
        gl0 = gh[CHUNK - 1:CHUNK, :]
        gl1 = gh[2 * CHUNK - 1:2 * CHUNK, :]
        rf0 = jnp.where(lm0, jnp.exp(jnp.where(lm0, gl0 - rowp, 0.0)), 0.0)
        rf1 = jnp.where(lm0, 0.0, jnp.exp(jnp.where(lm0, 0.0, gl1 - rowp)))
        kdt.append(((kt * rf0).astype(BF), (kt * rf1).astype(BF)))
        egl.append((jnp.exp(gl0), jnp.exp(gl1)))

    mneg = [jnp.where(strict, -(kq[i][:CHUNK] * dec[i]), 0.0) for i in nit]
    attn = [kq[i][CHUNK:] * dec[i] for i in nit]
    pinv = [eye_p + mneg[i] for i in nit]
    mp = [mm(mneg[i].astype(BF), bd(mneg[i])) for i in nit]
    for _ in range(4):
        r = [mm(jnp.concatenate([pinv[i], mp[i]], axis=0).astype(BF), bd(mp[i])) for i in nit]
        pinv = [pinv[i] + r[i][:CHUNK] for i in nit]
        mp = [r[i][CHUNK:] for i in nit]
    pinv = [pinv[i] + mm(pinv[i].astype(BF), bd(mp[i])) for i in nit]
    sol = [mm(bd(pinv[i]), rhs[i]) for i in nit]

    st = [state_ref[h] for h in heads]
    o_parts = [[] for _ in heads]
    zeros_c = jnp.zeros((CHUNK, GDN_DIM), BF)
    for blk in range(GDN_NB):
        for c in range(2):
            rs = slice(CHUNK * c, CHUNK * (c + 1))
            r1 = [mm(jnp.concatenate([sol[blk * GDN_HEADS + h][rs, GDN_DIM:].astype(BF),
                                      qd[blk * GDN_HEADS + h][rs]], axis=0), st[h].astype(BF))
                  for h in heads]
            for h in heads:
                i = blk * GDN_HEADS + h
                vn = (sol[i][rs, :GDN_DIM] - r1[h][:CHUNK]).astype(BF)
                vfull = jnp.concatenate([vn, zeros_c] if c == 0 else [zeros_c, vn], axis=0)
                am = jnp.where(lm0, attn[i], 0.0) if c == 0 else jnp.where(lm0, 0.0, attn[i])
                r2 = mm(jnp.concatenate([am.astype(BF), kdt[i][c]], axis=0), vfull)
                o_parts[h].append(r1[h][CHUNK:] + r2[:CHUNK])
                st[h] = st[h] * egl[i][c] + r2[CHUNK:]
    for h in heads:
        hs = slice(GDN_DIM * h, GDN_DIM * (h + 1))
        state_ref[h] = st[h]
        o_h = _rms(jnp.concatenate(o_parts[h], axis=0), og_ref[...])
        gt = gate_ref[:, hs].astype(F32)
        o_ref[:, hs] = (o_h * (gt * jax.nn.sigmoid(gt))).astype(BF)


def _gdn(p, g, cw, arow, dtrow, og, batch, seq):
    n = p.shape[0]
    T = GDN_T
    nt = seq // T
    tok = lambda col: pl.BlockSpec((T, GDN_W), lambda b, t: (b * nt + t, col // GDN_W))
    halo = lambda col: pl.BlockSpec(
        (8, GDN_W), lambda b, t: (jnp.maximum((b * nt + t) * (T // 8) - 1, 0), col // GDN_W))
    full = lambda r, c: pl.BlockSpec((r, c), lambda b, t: (0, 0))
    return pl.pallas_call(
        _gdn_kernel,
        grid=(batch, nt),
        in_specs=[
            tok(P_GQ), tok(P_GK), tok(P_GV), halo(P_GQ), halo(P_GK), halo(P_GV),
            pl.BlockSpec((T, 128), lambda b, t: (b * nt + t, 0)),
            tok(P_GG),
            full(CONV_W, 3 * GDN_W), full(1, 128), full(1, 128), full(1, GDN_DIM),
        ],
        out_specs=pl.BlockSpec((T, GDN_W), lambda b, t: (b * nt + t, 0)),
        out_shape=jax.ShapeDtypeStruct((n, GDN_W), BF),
        scratch_shapes=[
            pltpu.VMEM((GDN_HEADS, GDN_DIM, GDN_DIM), F32),
        ],
        compiler_params=_cparams(("parallel", "arbitrary")),
        name="gdn",
    )(p, p, p, p, p, p, g, p, cw, arow, dtrow, og)


OUT_TM, OUT_TN = 512, 512


def _outproj_kernel(a_ref, b_ref, wa_ref, wb_ref, x_ref, o_ref):
    a = a_ref[...]
    b = b_ref[...]
    for c in range(D_MODEL // OUT_TN):
        cs = slice(OUT_TN * c, OUT_TN * (c + 1))
        acc = jnp.dot(a, wa_ref[:, cs], preferred_element_type=F32)
        acc = acc + jnp.dot(b, wb_ref[:, cs], preferred_element_type=F32)
        o_ref[:, cs] = x_ref[:, cs] + acc


def _out_proj(a, b, wa, wb, x2):
    n = x2.shape[0]
    half = a.shape[1]
    return pl.pallas_call(
        _outproj_kernel,
        grid=(n // OUT_TM,),
        in_specs=[
            pl.BlockSpec((OUT_TM, half), lambda i: (i, 0)),
            pl.BlockSpec((OUT_TM, half), lambda i: (i, 0)),
            pl.BlockSpec((half, D_MODEL), lambda i: (0, 0)),
            pl.BlockSpec((half, D_MODEL), lambda i: (0, 0)),
            pl.BlockSpec((OUT_TM, D_MODEL), lambda i: (i, 0)),
        ],
        out_specs=pl.BlockSpec((OUT_TM, D_MODEL), lambda i: (i, 0)),
        out_shape=jax.ShapeDtypeStruct((n, D_MODEL), F32),
        compiler_params=_cparams(("parallel",)),
        name="out_proj",
    )(a, b, wa, wb, x2)


def _rot_cols(w):
    return jnp.concatenate([-w[..., ROPE // 2:], w[..., :ROPE // 2]], axis=-1)


def _swap_halves(g):
    return jnp.concatenate([g[..., ROPE // 2:], g[..., :ROPE // 2]], axis=-1)


def _prep_w_in(w):
    wb = w.astype(BF)
    cq, ckv, kr, mg, gq, gk, gv, ga, gb, gg = jnp.split(
        wb, [512, 768, 832, 1856, 2880, 3904, 4928, 4936, 4944], axis=1)
    pad = jnp.zeros((w.shape[0], P_COLS - P_G - 16), BF)
    return jnp.concatenate([gq, gk, gv, gg, mg, cq, ckv, kr, _rot_cols(kr), ga, gb, pad], axis=1)


def _prep_w_uq(w):
    w3 = w.reshape(Q_LORA, MLA_HEADS, QK_DIM)
    rope = w3[..., NOPE:]
    return jnp.concatenate([w3, _rot_cols(rope)], axis=-1).reshape(Q_LORA, MLA_HEADS * HEAD_SLAB).astype(BF)


def _prep_qk_gain(g):
    rope = g[NOPE:]
    return jnp.concatenate([g, _swap_halves(rope)])[None, :]


def kernel(x, positions, norm_gain, w_in, mla_q_a_gain, mla_kv_a_gain, w_uq, w_ukv,
           mla_q_norm_gain, mla_k_norm_gain, gdn_conv_w, gdn_a_log, gdn_dt_bias,
           gdn_out_norm_gain, w_out):
    batch, seq, _ = x.shape
    n = batch * seq
    x2 = x.reshape(n, D_MODEL)
    pos = positions.reshape(n, 1)
    half = ROPE // 2
    inv_freq = jnp.power(ROPE_THETA, -jnp.arange(half, dtype=F32) / half)
    invf = jnp.tile(inv_freq, 4)[None, :]

    h = x2
    for layer in range(w_in.shape[0]):
        p, g, cs = _in_proj(h, norm_gain[layer][None, :], _prep_w_in(w_in[layer]), pos, invf)
        qscale = QK_DIM ** -0.5 * LOG2E
        qp, kp, vp = _mla_prep(
            p, cs, mla_q_a_gain[layer][None, :], mla_kv_a_gain[layer][None, :],
            _prep_w_uq(w_uq[layer]), w_ukv[layer].astype(BF),
            _prep_qk_gain(mla_q_norm_gain[layer]) * qscale, _prep_qk_gain(mla_k_norm_gain[layer]))
        o_mla = _mla_attn(qp, kp, vp, p, batch, seq)
        zpad = jnp.zeros((128 - GDN_HEADS,), F32)
        arow = jnp.concatenate([gdn_a_log[layer].astype(F32), zpad])[None, :]
        dtrow = jnp.concatenate([gdn_dt_bias[layer].astype(F32), zpad])[None, :]
        o_gdn = _gdn(p, g, gdn_conv_w[layer], arow, dtrow, gdn_out_norm_gain[layer][None, :], batch, seq)
        wo = w_out[layer].astype(BF)
        h = _out_proj(o_mla, o_gdn, wo[:MLA_HEADS * V_DIM], wo[MLA_HEADS * V_DIM:], h)
    return h.reshape(batch, seq, D_MODEL)
```

```python
import functools
import math

import jax
import jax.numpy as jnp
from jax import lax
from jax.experimental import pallas as pl
from jax.experimental.pallas import tpu as pltpu

BF = jnp.bfloat16
F32 = jnp.float32

D_MODEL = 2048
MLA_HEADS = 8
NOPE = 128
ROPE = 64
QK_DIM = NOPE + ROPE
V_DIM = 128
Q_LORA = 512
KV_LORA = 256
GDN_HEADS = 8
GDN_DIM = 128
GDN_W = GDN_HEADS * GDN_DIM
CONV_W = 4
CHUNK = 64
LOG2_CHUNK = CHUNK.bit_length() - 1
LANES = 128
LOG2_LANES = LANES.bit_length() - 1
ROPE_THETA = 10000.0
EPS = 1e-6
LOG2E = math.log2(math.e)

P_GQ, P_GK, P_GV, P_GG, P_MG = 0, 1024, 2048, 3072, 4096
P_CQ, P_CKV, P_KR, P_G = 5120, 5632, 5888, 6016
P_COLS = 6144

VMEM_LIMIT = 56 * 1024 * 1024


def _cparams(sem):
    return pltpu.CompilerParams(dimension_semantics=sem, vmem_limit_bytes=VMEM_LIMIT)


def _silu(x):
    h = 0.5 * x
    return h + h * jnp.tanh(h)


IN_TM, IN_TN = 1024, 1536
IN_NJ = P_COLS // IN_TN
IN_ROPE_ROWS = IN_TM // IN_NJ


def _inproj_kernel(x_ref, gain_ref, w_ref, pos_ref, invf_ref, p_ref, g_ref, cs_ref, xn_ref):
    j = pl.program_id(1)

    @pl.when(j == 0)
    def _():
        xf = x_ref[...]
        ms = jnp.mean(xf * xf, axis=-1, keepdims=True)
        xn_ref[...] = (xf * lax.rsqrt(ms + EPS) * gain_ref[...]).astype(BF)

    acc = jnp.dot(xn_ref[...], w_ref[...], preferred_element_type=F32)
    p_ref[...] = acc.astype(BF)

    rows = pl.ds(pl.multiple_of(j * IN_ROPE_ROWS, IN_ROPE_ROWS), IN_ROPE_ROWS)
    lane = lax.broadcasted_iota(jnp.int32, (1, 128), 1)
    ang = pos_ref[rows, :].astype(F32) * invf_ref[...]
    cs_ref[rows, :] = jnp.cos(ang - jnp.where(lane < ROPE, 0.0, 0.5 * math.pi))

    @pl.when(j == IN_NJ - 1)
    def _():
        g_ref[...] = acc[:, IN_TN - 128:]


def _in_proj(x2, gain, w, pos, invf):
    n = x2.shape[0]
    return pl.pallas_call(
        _inproj_kernel,
        grid=(n // IN_TM, IN_NJ),
        in_specs=[
            pl.BlockSpec((IN_TM, D_MODEL), lambda i, j: (i, 0)),
            pl.BlockSpec((1, D_MODEL), lambda i, j: (0, 0)),
            pl.BlockSpec((D_MODEL, IN_TN), lambda i, j: (0, j)),
            pl.BlockSpec((IN_TM, 1), lambda i, j: (i, 0)),
            pl.BlockSpec((1, 128), lambda i, j: (0, 0)),
        ],
        out_specs=[
            pl.BlockSpec((IN_TM, IN_TN), lambda i, j: (i, j)),
            pl.BlockSpec((IN_TM, 128), lambda i, j: (i, 0)),
            pl.BlockSpec((IN_TM, 128), lambda i, j: (i, 0)),
        ],
        out_shape=[
            jax.ShapeDtypeStruct((n, P_COLS), BF),
            jax.ShapeDtypeStruct((n, 128), F32),
            jax.ShapeDtypeStruct((n, 128), F32),
        ],
        scratch_shapes=[pltpu.VMEM((IN_TM, D_MODEL), BF)],
        compiler_params=_cparams(("parallel", "arbitrary")),
        name="in_proj",
    )(x2, gain, w, pos, invf)


PREP_TM = 512
HEAD_SLAB = 256


def _rms(x, gain):
    ms = jnp.mean(x * x, axis=-1, keepdims=True)
    return x * lax.rsqrt(ms + EPS) * gain


def _mla_prep_kernel(cq_ref, ckv_ref, kr_ref, cs_ref, qag_ref, kvag_ref, wuq_ref, wukv_ref,
                     qg_ref, kg_ref, q_ref, k_ref, v_ref):
    lane = lax.broadcasted_iota(jnp.int32, (1, 128), 1)
    lo64 = lane < ROPE
    cs = cs_ref[...]

    cqn = _rms(cq_ref[...].astype(F32), qag_ref[...]).astype(BF)
    y = jnp.dot(cqn, wuq_ref[...], preferred_element_type=F32)
    qg0 = qg_ref[:, :128]
    qgcs = qg_ref[:, 128:] * cs
    for h in range(MLA_HEADS):
        y0 = y[:, HEAD_SLAB * h: HEAD_SLAB * h + 128]
        y1 = y[:, HEAD_SLAB * h + 128: HEAD_SLAB * (h + 1)]
        ss = jnp.sum(y0 * y0 + jnp.where(lo64, y1 * y1, 0.0), axis=-1, keepdims=True)
        r = lax.rsqrt(ss + QK_DIM * EPS)
        q_ref[:, HEAD_SLAB * h: HEAD_SLAB * h + 128] = (y0 * r * qg0).astype(BF)
        q_ref[:, HEAD_SLAB * h + 128: HEAD_SLAB * (h + 1)] = (y1 * r * qgcs).astype(BF)

    ckvn = _rms(ckv_ref[...].astype(F32), kvag_ref[...]).astype(BF)
    kv = jnp.dot(ckvn, wukv_ref[...], preferred_element_type=F32)
    kr = kr_ref[...].astype(F32)
    ssr = jnp.sum(jnp.where(lo64, kr * kr, 0.0), axis=-1, keepdims=True)
    kg0 = kg_ref[:, :128]
    kg1 = kg_ref[:, 128:]
    zk0 = kr * kg1 * cs
    zks = zk0 + pltpu.roll(zk0, ROPE, axis=1)
    for h in range(MLA_HEADS):
        kn = kv[:, HEAD_SLAB * h: HEAD_SLAB * h + 128]
        ss = jnp.sum(kn * kn, axis=-1, keepdims=True) + ssr
        r = lax.rsqrt(ss + QK_DIM * EPS)
        k_ref[:, HEAD_SLAB * h: HEAD_SLAB * h + 128] = (kn * r * kg0).astype(BF)
        k_ref[:, HEAD_SLAB * h + 128: HEAD_SLAB * (h + 1)] = (zks * r).astype(BF)
        v_ref[:, V_DIM * h: V_DIM * (h + 1)] = kv[:, HEAD_SLAB * h + 128: HEAD_SLAB * (h + 1)].astype(BF)


def _mla_prep(p, cs, qag, kvag, wuq, wukv, qg, kg):
    n = p.shape[0]
    tm = PREP_TM
    full = lambda r, c: pl.BlockSpec((r, c), lambda i: (0, 0))
    return pl.pallas_call(
        _mla_prep_kernel,
        grid=(n // tm,),
        in_specs=[
            pl.BlockSpec((tm, Q_LORA), lambda i: (i, P_CQ // Q_LORA)),
            pl.BlockSpec((tm, KV_LORA), lambda i: (i, P_CKV // KV_LORA)),
            pl.BlockSpec((tm, 128), lambda i: (i, P_KR // 128)),
            pl.BlockSpec((tm, 128), lambda i: (i, 0)),
            full(1, Q_LORA), full(1, KV_LORA),
            full(Q_LORA, MLA_HEADS * HEAD_SLAB), full(KV_LORA, MLA_HEADS * HEAD_SLAB),
            full(1, HEAD_SLAB), full(1, HEAD_SLAB),
        ],
        out_specs=[
            pl.BlockSpec((tm, MLA_HEADS * HEAD_SLAB), lambda i: (i, 0)),
            pl.BlockSpec((tm, MLA_HEADS * HEAD_SLAB), lambda i: (i, 0)),
            pl.BlockSpec((tm, MLA_HEADS * V_DIM), lambda i: (i, 0)),
        ],
        out_shape=[
            jax.ShapeDtypeStruct((n, MLA_HEADS * HEAD_SLAB), BF),
            jax.ShapeDtypeStruct((n, MLA_HEADS * HEAD_SLAB), BF),
            jax.ShapeDtypeStruct((n, MLA_HEADS * V_DIM), BF),
        ],
        compiler_params=_cparams(("parallel",)),
        name="mla_prep",
    )(p, p, p, cs, qag, kvag, wuq, wukv, qg, kg)


ATT_TQ = 512
ATT_TKW = 1024
ATT_HG = 4
ATT_NS = ATT_HG
NEG = -1e30


def _attn_kernel(q_ref, k_ref, v_ref, gate_ref, o_ref, m_ref, acc_ref):
    qi = pl.program_id(2)
    m_ref[...] = jnp.full(m_ref.shape, NEG, F32)
    acc_ref[...] = jnp.zeros(acc_ref.shape, F32)
    ones_col = {
        tk: jnp.where(lax.broadcasted_iota(jnp.int32, (tk, 128), 1) == 0, 1.0, 0.0).astype(BF)
        for tk in (ATT_TQ, ATT_TKW)}

    def scores(hh, off, tk):
        q = q_ref[:, HEAD_SLAB * hh: HEAD_SLAB * (hh + 1)]
        kj = k_ref[pl.ds(off, tk), HEAD_SLAB * hh: HEAD_SLAB * (hh + 1)]
        return lax.dot_general(q, kj, (((1,), (1,)), ((), ())), preferred_element_type=F32)

    def update(hh, off, tk, s, masked):
        vj = v_ref[pl.ds(off, tk), V_DIM * hh: V_DIM * (hh + 1)]
        if masked:
            row = lax.broadcasted_iota(jnp.int32, s.shape, 0) + qi * ATT_TQ
            col = lax.broadcasted_iota(jnp.int32, s.shape, 1) + off
            s = jnp.where(col <= row, s, NEG)
        m_prev = m_ref[hh]
        m_new = jnp.maximum(m_prev, jnp.max(s, axis=-1, keepdims=True))
        alpha = jnp.exp2(m_prev - m_new)
        p = jnp.exp2((s - jnp.concatenate([m_new] * (tk // 128), axis=1)).astype(BF))
        pv = jnp.dot(p, jnp.concatenate([vj, ones_col[tk]], axis=1), preferred_element_type=F32)
        acc_ref[hh] = jnp.concatenate([alpha, alpha], axis=1) * acc_ref[hh] + pv
        m_ref[hh] = m_new

    def run(off, tk, masked):
        ss = [scores(hh, off, tk) for hh in range(ATT_HG)]
        for hh, s in enumerate(ss):
            update(hh, off, tk, s, masked)

    def body(t, carry):
        run(pl.multiple_of(t * ATT_TKW, ATT_TKW), ATT_TKW, False)
        return carry

    assert ATT_TKW == 2 * ATT_TQ
    lax.fori_loop(0, lax.shift_right_logical(qi, 1), body, 0)

    @pl.when(jnp.bitwise_and(qi, 1) == 1)
    def _():
        run(pl.multiple_of((qi - 1) * ATT_TQ, ATT_TQ), ATT_TQ, False)

    run(pl.multiple_of(qi * ATT_TQ, ATT_TQ), ATT_TQ, True)

    for hh in range(ATT_HG):
        acc = acc_ref[hh]
        l = jnp.sum(acc[:, V_DIM:], axis=-1, keepdims=True)
        g = gate_ref[:, V_DIM * hh: V_DIM * (hh + 1)].astype(F32)
        o_ref[:, V_DIM * hh: V_DIM * (hh + 1)] = (
            acc[:, :V_DIM] * (1.0 / l) * _silu(g)).astype(BF)


def _mla_attn(qp, kp, vp, p, batch, seq):
    n = qp.shape[0]
    nq = seq // ATT_TQ
    hg = ATT_HG
    return pl.pallas_call(
        _attn_kernel,
        grid=(batch, MLA_HEADS // hg, nq),
        in_specs=[
            pl.BlockSpec((ATT_TQ, hg * HEAD_SLAB), lambda b, h, i: (b * nq + i, h)),
            pl.BlockSpec((seq, hg * HEAD_SLAB), lambda b, h, i: (b, h), pipeline_mode=pl.Buffered(1)),
            pl.BlockSpec((seq, hg * V_DIM), lambda b, h, i: (b, h)),
            pl.BlockSpec((ATT_TQ, hg * V_DIM), lambda b, h, i: (b * nq + i, P_MG // (hg * V_DIM) + h)),
        ],
        out_specs=pl.BlockSpec((ATT_TQ, hg * V_DIM), lambda b, h, i: (b * nq + i, h)),
        out_shape=jax.ShapeDtypeStruct((n, MLA_HEADS * V_DIM), BF),
        scratch_shapes=[
            pltpu.VMEM((ATT_NS, ATT_TQ, 128), F32),
            pltpu.VMEM((ATT_NS, ATT_TQ, 2 * V_DIM), F32),
        ],
        compiler_params=_cparams(("parallel", "parallel", "arbitrary")),
        name="mla_attn",
    )(qp, kp, vp, p)


GDN_T = 512
GDN_NB = GDN_T // 128


def _split3(x):
    a = x.astype(BF)
    r1 = x - a.astype(F32)
    b = r1.astype(BF)
    c = (r1 - b.astype(F32)).astype(BF)
    return a, b, c


def _dot_nt(a, b):
    return lax.dot_general(a, b, (((1,), (1,)), ((), ())), preferred_element_type=F32)


def _dot_tn(a, b):
    return lax.dot_general(a, b, (((0,), (0,)), ((), ())), preferred_element_type=F32)


def _gdn_kernel(q_ref, k_ref, v_ref, hq_ref, hk_ref, hv_ref, g_ref, gate_ref,
                cw_ref, arow_ref, dtrow_ref, og_ref, o_ref,
                state_ref):
    t = pl.program_id(1)
    T = GDN_T

    @pl.when(t == 0)
    def _():
        state_ref[...] = jnp.zeros(state_ref.shape, F32)

    first = t == 0

    sr = lax.broadcasted_iota(jnp.int32, (128, CONV_W * 128), 0)
    sc = lax.broadcasted_iota(jnp.int32, (128, CONV_W * 128), 1)
    shift_cat = jnp.where(jnp.bitwise_and(sc, LANES - 1) == sr - (CONV_W - 1) + jnp.right_shift(sc, LOG2_LANES),
                          1.0, 0.0).astype(BF)
    row8 = lax.broadcasted_iota(jnp.int32, (8, 1), 0)

    def conv_silu(x_ref, h_ref, col0):
        w = [cw_ref[j:j + 1, col0:col0 + GDN_W] for j in range(CONV_W)]
        wb = [wj.astype(BF) for wj in w]
        hz = jnp.where(first, 0.0, h_ref[...].astype(F32))
        ys = []
        for blk in range(GDN_NB):
            xb = x_ref[128 * blk:128 * (blk + 1), :]
            taps = jnp.concatenate([xb * wb[j] for j in range(CONV_W)], axis=0)
            y = jnp.dot(shift_cat, taps, preferred_element_type=F32)
            corr = jnp.zeros((8, GDN_W), F32)
            for j in range(CONV_W - 1):
                sh = CONV_W - 1 - j
                corr = corr + jnp.where(row8 < sh, pltpu.roll(hz, sh, axis=0), 0.0) * w[j]
            ys += [y[:8] + corr, y[8:]]
            hz = xb[112:].astype(F32)[8:]
        y = jnp.concatenate(ys, axis=0)
        return _silu(y)

    qc = conv_silu(q_ref, hq_ref, 0)
    kc = conv_silu(k_ref, hk_ref, GDN_W)
    vc = conv_silu(v_ref, hv_ref, 2 * GDN_W)

    gin = g_ref[...]
    gx = gin + dtrow_ref[...]
    softplus = jnp.maximum(gx, 0.0) + jnp.log1p(jnp.exp(-jnp.abs(gx)))
    gdec = -jnp.exp(arow_ref[...]) * softplus
    beta = jax.nn.sigmoid(gin)

    ri = lax.broadcasted_iota(jnp.int32, (T, T), 0)
    ci = lax.broadcasted_iota(jnp.int32, (T, T), 1)
    same_chunk = jnp.right_shift(ri, LOG2_CHUNK) == jnp.right_shift(ci, LOG2_CHUNK)
    tri = jnp.where(same_chunk, jnp.where(ci <= ri, 1.0, 0.0), 0.0).astype(BF)
    g1, g2, g3 = _split3(gdec)
    gc = (jnp.dot(tri, g1, preferred_element_type=F32)
          + jnp.dot(tri, g2, preferred_element_type=F32)
          + jnp.dot(tri, g3, preferred_element_type=F32))

    er = lax.broadcasted_iota(jnp.int32, (128, GDN_W), 0)
    ec = lax.broadcasted_iota(jnp.int32, (128, GDN_W), 1)
    e_g = jnp.where(er == jnp.right_shift(ec, LOG2_LANES), 1.0, 0.0).astype(BF)
    e_b = jnp.where(er == jnp.right_shift(ec, LOG2_LANES) + GDN_HEADS, 1.0, 0.0).astype(BF)
    c1, c2, _ = _split3(gc)
    gcb = (jnp.dot(c1, e_g, preferred_element_type=F32)
           + jnp.dot(c2, e_g, preferred_element_type=F32))
    betab = jnp.dot(beta.astype(BF), e_b, preferred_element_type=F32)
    gct = gc.T

    lane = lax.broadcasted_iota(jnp.int32, (1, 128), 1)
    lm0 = lane < CHUNK
    ii = lax.broadcasted_iota(jnp.int32, (CHUNK, 128), 0)
    jm = jnp.bitwise_and(lax.broadcasted_iota(jnp.int32, (CHUNK, 128), 1), CHUNK - 1)
    low = ii >= jm
    strict = ii > jm
    eye_p = jnp.where(ii == jm, 1.0, 0.0)
    heads = range(GDN_HEADS)
    items = [(blk, h) for blk in range(GDN_NB) for h in heads]
    nit = range(len(items))

    def bd(xp):
        return jnp.concatenate([jnp.where(lm0, xp, 0.0), jnp.where(lm0, 0.0, xp)], axis=0).astype(BF)

    def mm(a, b):
        return jnp.dot(a, b, preferred_element_type=F32)

    qd, rhs, kdt, egl, kq, dec = [], [], [], [], [], []
    for blk, h in items:
        hs = slice(GDN_DIM * h, GDN_DIM * (h + 1))
        bs = slice(128 * blk, 128 * (blk + 1))
        qh = qc[bs, hs]
        kh = kc[bs, hs]
        qss = jnp.broadcast_to(jnp.sum(qh * qh, axis=-1, keepdims=True), qh.shape)
        kss = jnp.broadcast_to(jnp.sum(kh * kh, axis=-1, keepdims=True), kh.shape)
        qh = qh * (lax.rsqrt(qss + EPS) * (GDN_DIM ** -0.5))
        kh = kh * lax.rsqrt(kss + EPS)
        bh = betab[bs, hs]
        gh = gcb[bs, hs]
        egh = jnp.exp(gh)
        kbh = kh * bh
        kt = kh.T
        rowp = gct[h:h + 1, bs]
        lhs = jnp.concatenate(
            [jnp.concatenate([kbh[:CHUNK], kbh[CHUNK:]], axis=1),
             jnp.concatenate([qh[:CHUNK], qh[CHUNK:]], axis=1)], axis=0).astype(BF)
        bdt = jnp.concatenate([jnp.where(lm0, kt, 0.0), jnp.where(lm0, 0.0, kt)], axis=0).astype(BF)
        kq.append(mm(lhs, bdt))
        colp = jnp.where(lm0, gh[:CHUNK], gh[CHUNK:])
        dec.append(jnp.where(low, jnp.exp(jnp.where(low, colp - rowp, 0.0)), 0.0))
        rhs.append(jnp.concatenate([vc[bs, hs] * bh, kbh * egh], axis=1).astype(BF))
        qd.append((qh * egh).astype(BF))
        gl0 = gh[CHUNK - 1:CHUNK, :]
        gl1 = gh[2 * CHUNK - 1:2 * CHUNK, :]
        rf0 = jnp.where(lm0, jnp.exp(jnp.where(lm0, gl0 - rowp, 0.0)), 0.0)
        rf1 = jnp.where(lm0, 0.0, jnp.exp(jnp.where(lm0, 0.0, gl1 - rowp)))
        kdt.append(((kt * rf0).astype(BF), (kt * rf1).astype(BF)))
        egl.append((jnp.exp(gl0), jnp.exp(gl1)))

    mneg = [jnp.where(strict, -(kq[i][:CHUNK] * dec[i]), 0.0) for i in nit]
    attn = [kq[i][CHUNK:] * dec[i] for i in nit]
    pinv = [eye_p + mneg[i] for i in nit]
    mp = [mm(mneg[i].astype(BF), bd(mneg[i])) for i in nit]
    for _ in range(4):
        r = [mm(jnp.concatenate([pinv[i], mp[i]], axis=0).astype(BF), bd(mp[i])) for i in nit]
        pinv = [pinv[i] + r[i][:CHUNK] for i in nit]
        mp = [r[i][CHUNK:] for i in nit]
    pinv = [pinv[i] + mm(pinv[i].astype(BF), bd(mp[i])) for i in nit]
    sol = [mm(bd(pinv[i]), rhs[i]) for i in nit]

    st = [state_ref[h] for h in heads]
    o_parts = [[] for _ in heads]
    zeros_c = jnp.zeros((CHUNK, GDN_DIM), BF)
    for blk in range(GDN_NB):
        for c in range(2):
            rs = slice(CHUNK * c, CHUNK * (c + 1))
            r1 = [mm(jnp.concatenate([sol[blk * GDN_HEADS + h][rs, GDN_DIM:].astype(BF),
                                      qd[blk * GDN_HEADS + h][rs]], axis=0), st[h].astype(BF))
                  for h in heads]
            for h in heads:
                i = blk * GDN_HEADS + h
                vn = (sol[i][rs, :GDN_DIM] - r1[h][:CHUNK]).astype(BF)
                vfull = jnp.concatenate([vn, zeros_c] if c == 0 else [zeros_c, vn], axis=0)
                am = jnp.where(lm0, attn[i], 0.0) if c == 0 else jnp.where(lm0, 0.0, attn[i])
                r2 = mm(jnp.concatenate([am.astype(BF), kdt[i][c]], axis=0), vfull)
                o_parts[h].append(r1[h][CHUNK:] + r2[:CHUNK])
                st[h] = st[h] * egl[i][c] + r2[CHUNK:]
    for h in heads:
        hs = slice(GDN_DIM * h, GDN_DIM * (h + 1))
        state_ref[h] = st[h]
        o_h = _rms(jnp.concatenate(o_parts[h], axis=0), og_ref[...])
        gt = gate_ref[:, hs].astype(F32)
        o_ref[:, hs] = (o_h * _silu(gt)).astype(BF)


def _gdn(p, g, cw, arow, dtrow, og, batch, seq):
    n = p.shape[0]
    T = GDN_T
    nt = seq // T
    tok = lambda col: pl.BlockSpec((T, GDN_W), lambda b, t: (b * nt + t, col // GDN_W))
    halo = lambda col: pl.BlockSpec(
        (8, GDN_W), lambda b, t: (jnp.maximum((b * nt + t) * (T // 8) - 1, 0), col // GDN_W))
    full = lambda r, c: pl.BlockSpec((r, c), lambda b, t: (0, 0))
    return pl.pallas_call(
        _gdn_kernel,
        grid=(batch, nt),
        in_specs=[
            tok(P_GQ), tok(P_GK), tok(P_GV), halo(P_GQ), halo(P_GK), halo(P_GV),
            pl.BlockSpec((T, 128), lambda b, t: (b * nt + t, 0)),
            tok(P_GG),
            full(CONV_W, 3 * GDN_W), full(1, 128), full(1, 128), full(1, GDN_DIM),
        ],
        out_specs=pl.BlockSpec((T, GDN_W), lambda b, t: (b * nt + t, 0)),
        out_shape=jax.ShapeDtypeStruct((n, GDN_W), BF),
        scratch_shapes=[
            pltpu.VMEM((GDN_HEADS, GDN_DIM, GDN_DIM), F32),
        ],
        compiler_params=_cparams(("parallel", "arbitrary")),
        name="gdn",
    )(p, p, p, p, p, p, g, p, cw, arow, dtrow, og)


OUT_TM, OUT_TN = 512, 512


def _outproj_kernel(a_ref, b_ref, wa_ref, wb_ref, x_ref, o_ref):
    a = a_ref[...]
    b = b_ref[...]
    for c in range(D_MODEL // OUT_TN):
        cs = slice(OUT_TN * c, OUT_TN * (c + 1))
        acc = jnp.dot(a, wa_ref[:, cs], preferred_element_type=F32)
        acc = acc + jnp.dot(b, wb_ref[:, cs], preferred_element_type=F32)
        o_ref[:, cs] = x_ref[:, cs] + acc


def _out_proj(a, b, wa, wb, x2):
    n = x2.shape[0]
    half = a.shape[1]
    return pl.pallas_call(
        _outproj_kernel,
        grid=(n // OUT_TM,),
        in_specs=[
            pl.BlockSpec((OUT_TM, half), lambda i: (i, 0)),
            pl.BlockSpec((OUT_TM, half), lambda i: (i, 0)),
            pl.BlockSpec((half, D_MODEL), lambda i: (0, 0)),
            pl.BlockSpec((half, D_MODEL), lambda i: (0, 0)),
            pl.BlockSpec((OUT_TM, D_MODEL), lambda i: (i, 0)),
        ],
        out_specs=pl.BlockSpec((OUT_TM, D_MODEL), lambda i: (i, 0)),
        out_shape=jax.ShapeDtypeStruct((n, D_MODEL), F32),
        compiler_params=_cparams(("parallel",)),
        name="out_proj",
    )(a, b, wa, wb, x2)


def _rot_cols(w):
    return jnp.concatenate([-w[..., ROPE // 2:], w[..., :ROPE // 2]], axis=-1)


def _swap_halves(g):
    return jnp.concatenate([g[..., ROPE // 2:], g[..., :ROPE // 2]], axis=-1)


def _prep_w_in(w):
    wb = w.astype(BF)
    cq, ckv, kr, mg, gq, gk, gv, ga, gb, gg = jnp.split(
        wb, [512, 768, 832, 1856, 2880, 3904, 4928, 4936, 4944], axis=1)
    pad = jnp.zeros((w.shape[0], P_COLS - P_G - 16), BF)
    return jnp.concatenate([gq, gk, gv, gg, mg, cq, ckv, kr, _rot_cols(kr), ga, gb, pad], axis=1)


def _prep_w_uq(w):
    w3 = w.reshape(Q_LORA, MLA_HEADS, QK_DIM)
    rope = w3[..., NOPE:]
    return jnp.concatenate([w3, _rot_cols(rope)], axis=-1).reshape(Q_LORA, MLA_HEADS * HEAD_SLAB).astype(BF)


def _prep_qk_gain(g):
    rope = g[NOPE:]
    return jnp.concatenate([g, _swap_halves(rope)])[None, :]


def kernel(x, positions, norm_gain, w_in, mla_q_a_gain, mla_kv_a_gain, w_uq, w_ukv,
           mla_q_norm_gain, mla_k_norm_gain, gdn_conv_w, gdn_a_log, gdn_dt_bias,
           gdn_out_norm_gain, w_out):
    batch, seq, _ = x.shape
    n = batch * seq
    x2 = x.reshape(n, D_MODEL)
    pos = positions.reshape(n, 1)
    half = ROPE // 2
    inv_freq = jnp.power(ROPE_THETA, -jnp.arange(half, dtype=F32) / half)
    invf = jnp.tile(inv_freq, 4)[None, :]

    h = x2
    for layer in range(w_in.shape[0]):
        p, g, cs = _in_proj(h, norm_gain[layer][None, :], _prep_w_in(w_in[layer]), pos, invf)
        qscale = QK_DIM ** -0.5 * LOG2E
        qp, kp, vp = _mla_prep(
            p, cs, mla_q_a_gain[layer][None, :], mla_kv_a_gain[layer][None, :],
            _prep_w_uq(w_uq[layer]), w_ukv[layer].astype(BF),
            _prep_qk_gain(mla_q_norm_gain[layer]) * (qscale * QK_DIM ** 0.5),
            _prep_qk_gain(mla_k_norm_gain[layer]) * QK_DIM ** 0.5)
        o_mla = _mla_attn(qp, kp, vp, p, batch, seq)
        zpad = jnp.zeros((128 - GDN_HEADS,), F32)
        arow = jnp.concatenate([gdn_a_log[layer].astype(F32), zpad])[None, :]
        dtrow = jnp.concatenate([gdn_dt_bias[layer].astype(F32), zpad])[None, :]
        o_gdn = _gdn(p, g, gdn_conv_w[layer], arow, dtrow, gdn_out_norm_gain[layer][None, :], batch, seq)
        wo = w_out[layer].astype(BF)
        h = _out_proj(o_mla, o_gdn, wo[:MLA_HEADS * V_DIM], wo[MLA_HEADS * V_DIM:], h)
    return h.reshape(batch, seq, D_MODEL)
```

```python
import functools
import math

import jax
import jax.numpy as jnp
from jax import lax
from jax.experimental import pallas as pl
from jax.experimental.pallas import tpu as pltpu

BF = jnp.bfloat16
F32 = jnp.float32

D_MODEL = 2048
MLA_HEADS = 8
NOPE = 128
ROPE = 64
QK_DIM = NOPE + ROPE
V_DIM = 128
Q_LORA = 512
KV_LORA = 256
GDN_HEADS = 8
GDN_DIM = 128
GDN_W = GDN_HEADS * GDN_DIM
CONV_W = 4
CHUNK = 64
LOG2_CHUNK = CHUNK.bit_length() - 1
LANES = 128
LOG2_LANES = LANES.bit_length() - 1
ROPE_THETA = 10000.0
EPS = 1e-6
LOG2E = math.log2(math.e)

P_GQ, P_GK, P_GV, P_GG, P_MG = 0, 1024, 2048, 3072, 4096
P_CQ, P_CKV, P_KR, P_G = 5120, 5632, 5888, 6016
P_COLS = 6144

VMEM_LIMIT = 56 * 1024 * 1024


def _cparams(sem):
    return pltpu.CompilerParams(dimension_semantics=sem, vmem_limit_bytes=VMEM_LIMIT)


def _silu(x):
    h = 0.5 * x
    return h + h * jnp.tanh(h)


IN_TM, IN_TN = 1024, 1536
IN_NJ = P_COLS // IN_TN
IN_ROPE_ROWS = IN_TM // IN_NJ


def _inproj_kernel(x_ref, gain_ref, w_ref, pos_ref, invf_ref, p_ref, g_ref, cs_ref, xn_ref):
    j = pl.program_id(1)

    @pl.when(j == 0)
    def _():
        xf = x_ref[...]
        ms = jnp.mean(xf * xf, axis=-1, keepdims=True)
        xn_ref[...] = (xf * lax.rsqrt(ms + EPS) * gain_ref[...]).astype(BF)

    acc = jnp.dot(xn_ref[...], w_ref[...], preferred_element_type=F32)
    p_ref[...] = acc.astype(BF)

    rows = pl.ds(pl.multiple_of(j * IN_ROPE_ROWS, IN_ROPE_ROWS), IN_ROPE_ROWS)
    lane = lax.broadcasted_iota(jnp.int32, (1, 128), 1)
    ang = pos_ref[rows, :].astype(F32) * invf_ref[...]
    cs_ref[rows, :] = jnp.cos(ang - jnp.where(lane < ROPE, 0.0, 0.5 * math.pi))

    @pl.when(j == IN_NJ - 1)
    def _():
        g_ref[...] = acc[:, IN_TN - 128:]


def _in_proj(x2, gain, w, pos, invf):
    n = x2.shape[0]
    return pl.pallas_call(
        _inproj_kernel,
        grid=(n // IN_TM, IN_NJ),
        in_specs=[
            pl.BlockSpec((IN_TM, D_MODEL), lambda i, j: (i, 0)),
            pl.BlockSpec((1, D_MODEL), lambda i, j: (0, 0)),
            pl.BlockSpec((D_MODEL, IN_TN), lambda i, j: (0, j)),
            pl.BlockSpec((IN_TM, 1), lambda i, j: (i, 0)),
            pl.BlockSpec((1, 128), lambda i, j: (0, 0)),
        ],
        out_specs=[
            pl.BlockSpec((IN_TM, IN_TN), lambda i, j: (i, j)),
            pl.BlockSpec((IN_TM, 128), lambda i, j: (i, 0)),
            pl.BlockSpec((IN_TM, 128), lambda i, j: (i, 0)),
        ],
        out_shape=[
            jax.ShapeDtypeStruct((n, P_COLS), BF),
            jax.ShapeDtypeStruct((n, 128), F32),
            jax.ShapeDtypeStruct((n, 128), F32),
        ],
        scratch_shapes=[pltpu.VMEM((IN_TM, D_MODEL), BF)],
        compiler_params=_cparams(("parallel", "arbitrary")),
        name="in_proj",
    )(x2, gain, w, pos, invf)


PREP_TM = 512
HEAD_SLAB = 256


def _rms(x, gain):
    ms = jnp.mean(x * x, axis=-1, keepdims=True)
    return x * lax.rsqrt(ms + EPS) * gain


def _mla_prep_kernel(cq_ref, ckv_ref, kr_ref, cs_ref, qag_ref, kvag_ref, wuq_ref, wukv_ref,
                     qg_ref, kg_ref, q_ref, k_ref, v_ref):
    lane = lax.broadcasted_iota(jnp.int32, (1, 128), 1)
    lo64 = lane < ROPE
    cs = cs_ref[...]

    cqn = _rms(cq_ref[...].astype(F32), qag_ref[...]).astype(BF)
    y = jnp.dot(cqn, wuq_ref[...], preferred_element_type=F32)
    qg0 = qg_ref[:, :128]
    qgcs = qg_ref[:, 128:] * cs
    for h in range(MLA_HEADS):
        y0 = y[:, HEAD_SLAB * h: HEAD_SLAB * h + 128]
        y1 = y[:, HEAD_SLAB * h + 128: HEAD_SLAB * (h + 1)]
        ss = jnp.sum(y0 * y0 + jnp.where(lo64, y1 * y1, 0.0), axis=-1, keepdims=True)
        r = lax.rsqrt(ss + QK_DIM * EPS)
        q_ref[:, HEAD_SLAB * h: HEAD_SLAB * h + 128] = (y0 * r * qg0).astype(BF)
        q_ref[:, HEAD_SLAB * h + 128: HEAD_SLAB * (h + 1)] = (y1 * r * qgcs).astype(BF)

    ckvn = _rms(ckv_ref[...].astype(F32), kvag_ref[...]).astype(BF)
    kv = jnp.dot(ckvn, wukv_ref[...], preferred_element_type=F32)
    kr = kr_ref[...].astype(F32)
    ssr = jnp.sum(jnp.where(lo64, kr * kr, 0.0), axis=-1, keepdims=True)
    kg0 = kg_ref[:, :128]
    kg1 = kg_ref[:, 128:]
    zk0 = kr * kg1 * cs
    zks = zk0 + pltpu.roll(zk0, ROPE, axis=1)
    for h in range(MLA_HEADS):
        kn = kv[:, HEAD_SLAB * h: HEAD_SLAB * h + 128]
        ss = jnp.sum(kn * kn, axis=-1, keepdims=True) + ssr
        r = lax.rsqrt(ss + QK_DIM * EPS)
        k_ref[:, HEAD_SLAB * h: HEAD_SLAB * h + 128] = (kn * r * kg0).astype(BF)
        k_ref[:, HEAD_SLAB * h + 128: HEAD_SLAB * (h + 1)] = (zks * r).astype(BF)
        v_ref[:, V_DIM * h: V_DIM * (h + 1)] = kv[:, HEAD_SLAB * h + 128: HEAD_SLAB * (h + 1)].astype(BF)


def _mla_prep(p, cs, qag, kvag, wuq, wukv, qg, kg):
    n = p.shape[0]
    tm = PREP_TM
    full = lambda r, c: pl.BlockSpec((r, c), lambda i: (0, 0))
    return pl.pallas_call(
        _mla_prep_kernel,
        grid=(n // tm,),
        in_specs=[
            pl.BlockSpec((tm, Q_LORA), lambda i: (i, P_CQ // Q_LORA)),
            pl.BlockSpec((tm, KV_LORA), lambda i: (i, P_CKV // KV_LORA)),
            pl.BlockSpec((tm, 128), lambda i: (i, P_KR // 128)),
            pl.BlockSpec((tm, 128), lambda i: (i, 0)),
            full(1, Q_LORA), full(1, KV_LORA),
            full(Q_LORA, MLA_HEADS * HEAD_SLAB), full(KV_LORA, MLA_HEADS * HEAD_SLAB),
            full(1, HEAD_SLAB), full(1, HEAD_SLAB),
        ],
        out_specs=[
            pl.BlockSpec((tm, MLA_HEADS * HEAD_SLAB), lambda i: (i, 0)),
            pl.BlockSpec((tm, MLA_HEADS * HEAD_SLAB), lambda i: (i, 0)),
            pl.BlockSpec((tm, MLA_HEADS * V_DIM), lambda i: (i, 0)),
        ],
        out_shape=[
            jax.ShapeDtypeStruct((n, MLA_HEADS * HEAD_SLAB), BF),
            jax.ShapeDtypeStruct((n, MLA_HEADS * HEAD_SLAB), BF),
            jax.ShapeDtypeStruct((n, MLA_HEADS * V_DIM), BF),
        ],
        compiler_params=_cparams(("parallel",)),
        name="mla_prep",
    )(p, p, p, cs, qag, kvag, wuq, wukv, qg, kg)


ATT_TQ = 512
ATT_TKW = 1024
ATT_HG = 4
ATT_NS = ATT_HG
NEG = -1e30


def _attn_kernel(q_ref, k_ref, v_ref, gate_ref, o_ref, m_ref, acc_ref):
    qi = pl.program_id(2)
    m_ref[...] = jnp.full(m_ref.shape, NEG, F32)
    acc_ref[...] = jnp.zeros(acc_ref.shape, F32)
    ones_col = {
        tk: jnp.where(lax.broadcasted_iota(jnp.int32, (tk, 128), 1) == 0, 1.0, 0.0).astype(BF)
        for tk in (ATT_TQ, ATT_TKW)}

    def scores(hh, off, tk):
        q = q_ref[:, HEAD_SLAB * hh: HEAD_SLAB * (hh + 1)]
        kj = k_ref[pl.ds(off, tk), HEAD_SLAB * hh: HEAD_SLAB * (hh + 1)]
        return lax.dot_general(q, kj, (((1,), (1,)), ((), ())), preferred_element_type=F32)

    def update(hh, off, tk, s, masked):
        vj = v_ref[pl.ds(off, tk), V_DIM * hh: V_DIM * (hh + 1)]
        if masked:
            row = lax.broadcasted_iota(jnp.int32, s.shape, 0) + qi * ATT_TQ
            col = lax.broadcasted_iota(jnp.int32, s.shape, 1) + off
            s = jnp.where(col <= row, s, NEG)
        m_prev = m_ref[hh]
        m_new = jnp.maximum(m_prev, jnp.max(s, axis=-1, keepdims=True))
        alpha = jnp.exp2(m_prev - m_new)
        p = jnp.exp2((s - jnp.concatenate([m_new] * (tk // 128), axis=1)).astype(BF))
        pv = jnp.dot(p, jnp.concatenate([vj, ones_col[tk]], axis=1), preferred_element_type=F32)
        acc_ref[hh] = jnp.concatenate([alpha, alpha], axis=1) * acc_ref[hh] + pv
        m_ref[hh] = m_new

    def run(off, tk, masked):
        ss = [scores(hh, off, tk) for hh in range(ATT_HG)]
        for hh, s in enumerate(ss):
            update(hh, off, tk, s, masked)

    def body(t, carry):
        run(pl.multiple_of(t * ATT_TKW, ATT_TKW), ATT_TKW, False)
        return carry

    assert ATT_TKW == 2 * ATT_TQ
    lax.fori_loop(0, lax.shift_right_logical(qi, 1), body, 0)

    @pl.when(jnp.bitwise_and(qi, 1) == 1)
    def _():
        run(pl.multiple_of((qi - 1) * ATT_TQ, ATT_TQ), ATT_TQ, False)

    run(pl.multiple_of(qi * ATT_TQ, ATT_TQ), ATT_TQ, True)

    for hh in range(ATT_HG):
        acc = acc_ref[hh]
        l = jnp.sum(acc[:, V_DIM:], axis=-1, keepdims=True)
        g = gate_ref[:, V_DIM * hh: V_DIM * (hh + 1)].astype(F32)
        o_ref[:, V_DIM * hh: V_DIM * (hh + 1)] = (
            acc[:, :V_DIM] * (1.0 / l) * _silu(g)).astype(BF)


def _mla_attn(qp, kp, vp, p, batch, seq):
    n = qp.shape[0]
    nq = seq // ATT_TQ
    hg = ATT_HG
    return pl.pallas_call(
        _attn_kernel,
        grid=(batch, MLA_HEADS // hg, nq),
        in_specs=[
            pl.BlockSpec((ATT_TQ, hg * HEAD_SLAB), lambda b, h, i: (b * nq + i, h)),
            pl.BlockSpec((seq, hg * HEAD_SLAB), lambda b, h, i: (b, h), pipeline_mode=pl.Buffered(1)),
            pl.BlockSpec((seq, hg * V_DIM), lambda b, h, i: (b, h)),
            pl.BlockSpec((ATT_TQ, hg * V_DIM), lambda b, h, i: (b * nq + i, P_MG // (hg * V_DIM) + h)),
        ],
        out_specs=pl.BlockSpec((ATT_TQ, hg * V_DIM), lambda b, h, i: (b * nq + i, h)),
        out_shape=jax.ShapeDtypeStruct((n, MLA_HEADS * V_DIM), BF),
        scratch_shapes=[
            pltpu.VMEM((ATT_NS, ATT_TQ, 128), F32),
            pltpu.VMEM((ATT_NS, ATT_TQ, 2 * V_DIM), F32),
        ],
        compiler_params=_cparams(("parallel", "parallel", "arbitrary")),
        name="mla_attn",
    )(qp, kp, vp, p)


GDN_T = 512
GDN_NB = GDN_T // 128


def _split3(x):
    a = x.astype(BF)
    r1 = x - a.astype(F32)
    b = r1.astype(BF)
    c = (r1 - b.astype(F32)).astype(BF)
    return a, b, c


def _dot_nt(a, b):
    return lax.dot_general(a, b, (((1,), (1,)), ((), ())), preferred_element_type=F32)


def _dot_tn(a, b):
    return lax.dot_general(a, b, (((0,), (0,)), ((), ())), preferred_element_type=F32)


def _gdn_kernel(q_ref, k_ref, v_ref, hq_ref, hk_ref, hv_ref, g_ref, gate_ref,
                cw_ref, arow_ref, dtrow_ref, og_ref, o_ref,
                state_ref):
    t = pl.program_id(1)
    T = GDN_T

    @pl.when(t == 0)
    def _():
        state_ref[...] = jnp.zeros(state_ref.shape, F32)

    first = t == 0

    sr = lax.broadcasted_iota(jnp.int32, (128, CONV_W * 128), 0)
    sc = lax.broadcasted_iota(jnp.int32, (128, CONV_W * 128), 1)
    shift_cat = jnp.where(jnp.bitwise_and(sc, LANES - 1) == sr - (CONV_W - 1) + jnp.right_shift(sc, LOG2_LANES),
                          1.0, 0.0).astype(BF)
    row8 = lax.broadcasted_iota(jnp.int32, (8, 1), 0)

    def conv_silu(x_ref, h_ref, col0):
        w = [cw_ref[j:j + 1, col0:col0 + GDN_W] for j in range(CONV_W)]
        wb = [wj.astype(BF) for wj in w]
        hz = jnp.where(first, 0.0, h_ref[...].astype(F32))
        ys = []
        for blk in range(GDN_NB):
            xb = x_ref[128 * blk:128 * (blk + 1), :]
            taps = jnp.concatenate([xb * wb[j] for j in range(CONV_W)], axis=0)
            y = jnp.dot(shift_cat, taps, preferred_element_type=F32)
            corr = jnp.zeros((8, GDN_W), F32)
            for j in range(CONV_W - 1):
                sh = CONV_W - 1 - j
                corr = corr + jnp.where(row8 < sh, pltpu.roll(hz, sh, axis=0), 0.0) * w[j]
            ys += [y[:8] + corr, y[8:]]
            hz = xb[112:].astype(F32)[8:]
        y = jnp.concatenate(ys, axis=0)
        return _silu(y)

    qc = conv_silu(q_ref, hq_ref, 0)
    kc = conv_silu(k_ref, hk_ref, GDN_W)
    vc = conv_silu(v_ref, hv_ref, 2 * GDN_W)

    gin = g_ref[...]
    gx = gin + dtrow_ref[...]
    softplus = jnp.maximum(gx, 0.0) + jnp.log1p(jnp.exp(-jnp.abs(gx)))
    gdec = -jnp.exp(arow_ref[...]) * softplus
    beta = jax.nn.sigmoid(gin)

    ri = lax.broadcasted_iota(jnp.int32, (T, T), 0)
    ci = lax.broadcasted_iota(jnp.int32, (T, T), 1)
    same_chunk = jnp.right_shift(ri, LOG2_CHUNK) == jnp.right_shift(ci, LOG2_CHUNK)
    tri = jnp.where(same_chunk, jnp.where(ci <= ri, 1.0, 0.0), 0.0).astype(BF)
    g1, g2, g3 = _split3(gdec)
    gc = (jnp.dot(tri, g1, preferred_element_type=F32)
          + jnp.dot(tri, g2, preferred_element_type=F32)
          + jnp.dot(tri, g3, preferred_element_type=F32))

    er = lax.broadcasted_iota(jnp.int32, (128, GDN_W), 0)
    ec = lax.broadcasted_iota(jnp.int32, (128, GDN_W), 1)
    e_g = jnp.where(er == jnp.right_shift(ec, LOG2_LANES), 1.0, 0.0).astype(BF)
    e_b = jnp.where(er == jnp.right_shift(ec, LOG2_LANES) + GDN_HEADS, 1.0, 0.0).astype(BF)
    c1, c2, _ = _split3(gc)
    gcb = (jnp.dot(c1, e_g, preferred_element_type=F32)
           + jnp.dot(c2, e_g, preferred_element_type=F32))
    betab = jnp.dot(beta.astype(BF), e_b, preferred_element_type=F32)
    gct = gc.T

    lane = lax.broadcasted_iota(jnp.int32, (1, 128), 1)
    lm0 = lane < CHUNK
    ii = lax.broadcasted_iota(jnp.int32, (CHUNK, 128), 0)
    jm = jnp.bitwise_and(lax.broadcasted_iota(jnp.int32, (CHUNK, 128), 1), CHUNK - 1)
    low = ii >= jm
    strict = ii > jm
    eye_p = jnp.where(ii == jm, 1.0, 0.0)
    heads = range(GDN_HEADS)
    items = [(blk, h) for blk in range(GDN_NB) for h in heads]
    nit = range(len(items))

    def bd(xp):
        return jnp.concatenate([jnp.where(lm0, xp, 0.0), jnp.where(lm0, 0.0, xp)], axis=0).astype(BF)

    def mm(a, b):
        return jnp.dot(a, b, preferred_element_type=F32)

    qd, rhs, kdt, egl, kq, dec = [], [], [], [], [], []
    for blk, h in items:
        hs = slice(GDN_DIM * h, GDN_DIM * (h + 1))
        bs = slice(128 * blk, 128 * (blk + 1))
        qh = qc[bs, hs]
        kh = kc[bs, hs]
        qss = jnp.broadcast_to(jnp.sum(qh * qh, axis=-1, keepdims=True), qh.shape)
        kss = jnp.broadcast_to(jnp.sum(kh * kh, axis=-1, keepdims=True), kh.shape)
        qh = qh * (lax.rsqrt(qss + EPS) * (GDN_DIM ** -0.5))
        kh = kh * lax.rsqrt(kss + EPS)
        bh = betab[bs, hs]
        gh = gcb[bs, hs]
        egh = jnp.exp(gh)
        kbh = kh * bh
        kt = kh.T
        rowp = gct[h:h + 1, bs]
        lhs = jnp.concatenate(
            [jnp.concatenate([kbh[:CHUNK], kbh[CHUNK:]], axis=1),
             jnp.concatenate([qh[:CHUNK], qh[CHUNK:]], axis=1)], axis=0).astype(BF)
        bdt = jnp.concatenate([jnp.where(lm0, kt, 0.0), jnp.where(lm0, 0.0, kt)], axis=0).astype(BF)
        kq.append(mm(lhs, bdt))
        colp = jnp.where(lm0, gh[:CHUNK], gh[CHUNK:])
        dec.append(jnp.where(low, jnp.exp(jnp.where(low, colp - rowp, 0.0)), 0.0))
        rhs.append(jnp.concatenate([vc[bs, hs] * bh, kbh * egh], axis=1).astype(BF))
        qd.append((qh * egh).astype(BF))
        gl0 = gh[CHUNK - 1:CHUNK, :]
        gl1 = gh[2 * CHUNK - 1:2 * CHUNK, :]
        rf0 = jnp.where(lm0, jnp.exp(jnp.where(lm0, gl0 - rowp, 0.0)), 0.0)
        rf1 = jnp.where(lm0, 0.0, jnp.exp(jnp.where(lm0, 0.0, gl1 - rowp)))
        kdt.append(((kt * rf0).astype(BF), (kt * rf1).astype(BF)))
        egl.append((jnp.exp(gl0), jnp.exp(gl1)))

    mneg = [jnp.where(strict, -(kq[i][:CHUNK] * dec[i]), 0.0) for i in nit]
    attn = [kq[i][CHUNK:] * dec[i] for i in nit]
    pinv = [eye_p + mneg[i] for i in nit]
    mp = [mm(mneg[i].astype(BF), bd(mneg[i])) for i in nit]
    for _ in range(4):
        r = [mm(jnp.concatenate([pinv[i], mp[i]], axis=0).astype(BF), bd(mp[i])) for i in nit]
        pinv = [pinv[i] + r[i][:CHUNK] for i in nit]
        mp = [r[i][CHUNK:] for i in nit]
    pinv = [pinv[i] + mm(pinv[i].astype(BF), bd(mp[i])) for i in nit]
    sol = [mm(bd(pinv[i]), rhs[i]) for i in nit]

    st = [state_ref[h] for h in heads]
    o_parts = [[] for _ in heads]
    zeros_c = jnp.zeros((CHUNK, GDN_DIM), BF)
    for blk in range(GDN_NB):
        for c in range(2):
            rs = slice(CHUNK * c, CHUNK * (c + 1))
            r1 = [mm(jnp.concatenate([sol[blk * GDN_HEADS + h][rs, GDN_DIM:].astype(BF),
                                      qd[blk * GDN_HEADS + h][rs]], axis=0), st[h].astype(BF))
                  for h in heads]
            for h in heads:
                i = blk * GDN_HEADS + h
                vn = (sol[i][rs, :GDN_DIM] - r1[h][:CHUNK]).astype(BF)
                vfull = jnp.concatenate([vn, zeros_c] if c == 0 else [zeros_c, vn], axis=0)
                am = jnp.where(lm0, attn[i], 0.0) if c == 0 else jnp.where(lm0, 0.0, attn[i])
                r2 = mm(jnp.concatenate([am.astype(BF), kdt[i][c]], axis=0), vfull)
                o_parts[h].append(r1[h][CHUNK:] + r2[:CHUNK])
                st[h] = st[h] * egl[i][c] + r2[CHUNK:]
    for h in heads:
        hs = slice(GDN_DIM * h, GDN_DIM * (h + 1))
        state_ref[h] = st[h]
        o_h = _rms(jnp.concatenate(o_parts[h], axis=0), og_ref[...])
        gt = gate_ref[:, hs].astype(F32)
        o_ref[:, hs] = (o_h * _silu(gt)).astype(BF)


def _gdn(p, g, cw, arow, dtrow, og, batch, seq):
    n = p.shape[0]
    T = GDN_T
    nt = seq // T
    tok = lambda col: pl.BlockSpec((T, GDN_W), lambda b, t: (b * nt + t, col // GDN_W))
    halo = lambda col: pl.BlockSpec(
        (8, GDN_W), lambda b, t: (jnp.maximum((b * nt + t) * (T // 8) - 1, 0), col // GDN_W))
    full = lambda r, c: pl.BlockSpec((r, c), lambda b, t: (0, 0))
    return pl.pallas_call(
        _gdn_kernel,
        grid=(batch, nt),
        in_specs=[
            tok(P_GQ), tok(P_GK), tok(P_GV), halo(P_GQ), halo(P_GK), halo(P_GV),
            pl.BlockSpec((T, 128), lambda b, t: (b * nt + t, 0)),
            tok(P_GG),
            full(CONV_W, 3 * GDN_W), full(1, 128), full(1, 128), full(1, GDN_DIM),
        ],
        out_specs=pl.BlockSpec((T, GDN_W), lambda b, t: (b * nt + t, 0)),
        out_shape=jax.ShapeDtypeStruct((n, GDN_W), BF),
        scratch_shapes=[
            pltpu.VMEM((GDN_HEADS, GDN_DIM, GDN_DIM), F32),
        ],
        compiler_params=_cparams(("parallel", "arbitrary")),
        name="gdn",
    )(p, p, p, p, p, p, g, p, cw, arow, dtrow, og)


OUT_TM, OUT_TN = 512, 512


def _outproj_kernel(a_ref, b_ref, wa_ref, wb_ref, x_ref, o_ref):
    a = a_ref[...]
    b = b_ref[...]
    for c in range(D_MODEL // OUT_TN):
        cs = slice(OUT_TN * c, OUT_TN * (c + 1))
        acc = jnp.dot(a, wa_ref[:, cs], preferred_element_type=F32)
        acc = acc + jnp.dot(b, wb_ref[:, cs], preferred_element_type=F32)
        o_ref[:, cs] = x_ref[:, cs] + acc


def _out_proj(a, b, w, x2):
    n = x2.shape[0]
    half = a.shape[1]
    return pl.pallas_call(
        _outproj_kernel,
        grid=(n // OUT_TM,),
        in_specs=[
            pl.BlockSpec((OUT_TM, half), lambda i: (i, 0)),
            pl.BlockSpec((OUT_TM, half), lambda i: (i, 0)),
            pl.BlockSpec((half, D_MODEL), lambda i: (0, 0)),
            pl.BlockSpec((half, D_MODEL), lambda i: (1, 0)),
            pl.BlockSpec((OUT_TM, D_MODEL), lambda i: (i, 0)),
        ],
        out_specs=pl.BlockSpec((OUT_TM, D_MODEL), lambda i: (i, 0)),
        out_shape=jax.ShapeDtypeStruct((n, D_MODEL), F32),
        compiler_params=_cparams(("parallel",)),
        name="out_proj",
    )(a, b, w, w, x2)


def _rot_cols(w):
    return jnp.concatenate([-w[..., ROPE // 2:], w[..., :ROPE // 2]], axis=-1)


def _swap_halves(g):
    return jnp.concatenate([g[..., ROPE // 2:], g[..., :ROPE // 2]], axis=-1)


def _prep_w_in(w):
    wb = w.astype(BF)
    cq, ckv, kr, mg, gq, gk, gv, ga, gb, gg = jnp.split(
        wb, [512, 768, 832, 1856, 2880, 3904, 4928, 4936, 4944], axis=1)
    pad = jnp.zeros((w.shape[0], P_COLS - P_G - 16), BF)
    return jnp.concatenate([gq, gk, gv, gg, mg, cq, ckv, kr, _rot_cols(kr), ga, gb, pad], axis=1)


def _prep_w_uq(w):
    w3 = w.reshape(Q_LORA, MLA_HEADS, QK_DIM)
    rope = w3[..., NOPE:]
    return jnp.concatenate([w3, _rot_cols(rope)], axis=-1).reshape(Q_LORA, MLA_HEADS * HEAD_SLAB).astype(BF)


def _prep_qk_gain(g):
    rope = g[NOPE:]
    return jnp.concatenate([g, _swap_halves(rope)])[None, :]


def kernel(x, positions, norm_gain, w_in, mla_q_a_gain, mla_kv_a_gain, w_uq, w_ukv,
           mla_q_norm_gain, mla_k_norm_gain, gdn_conv_w, gdn_a_log, gdn_dt_bias,
           gdn_out_norm_gain, w_out):
    batch, seq, _ = x.shape
    n = batch * seq
    x2 = x.reshape(n, D_MODEL)
    pos = positions.reshape(n, 1)
    half = ROPE // 2
    inv_freq = jnp.power(ROPE_THETA, -jnp.arange(half, dtype=F32) / half)
    invf = jnp.tile(inv_freq, 4)[None, :]

    h = x2
    for layer in range(w_in.shape[0]):
        p, g, cs = _in_proj(h, norm_gain[layer][None, :], _prep_w_in(w_in[layer]), pos, invf)
        qscale = QK_DIM ** -0.5 * LOG2E
        qp, kp, vp = _mla_prep(
            p, cs, mla_q_a_gain[layer][None, :], mla_kv_a_gain[layer][None, :],
            _prep_w_uq(w_uq[layer]), w_ukv[layer].astype(BF),
            _prep_qk_gain(mla_q_norm_gain[layer]) * (qscale * QK_DIM ** 0.5),
            _prep_qk_gain(mla_k_norm_gain[layer]) * QK_DIM ** 0.5)
        o_mla = _mla_attn(qp, kp, vp, p, batch, seq)
        zpad = jnp.zeros((128 - GDN_HEADS,), F32)
        arow = jnp.concatenate([gdn_a_log[layer].astype(F32), zpad])[None, :]
        dtrow = jnp.concatenate([gdn_dt_bias[layer].astype(F32), zpad])[None, :]
        o_gdn = _gdn(p, g, gdn_conv_w[layer], arow, dtrow, gdn_out_norm_gain[layer][None, :], batch, seq)
        h = _out_proj(o_mla, o_gdn, w_out[layer].astype(BF), h)
    return h.reshape(batch, seq, D_MODEL)
```

```python
import functools
import math

import jax
import jax.numpy as jnp
from jax import lax
from jax.experimental import pallas as pl
from jax.experimental.pallas import tpu as pltpu

BF = jnp.bfloat16
F32 = jnp.float32

D_MODEL = 2048
MLA_HEADS = 8
NOPE = 128
ROPE = 64
QK_DIM = NOPE + ROPE
V_DIM = 128
Q_LORA = 512
KV_LORA = 256
GDN_HEADS = 8
GDN_DIM = 128
GDN_W = GDN_HEADS * GDN_DIM
CONV_W = 4
CHUNK = 64
LOG2_CHUNK = CHUNK.bit_length() - 1
LANES = 128
LOG2_LANES = LANES.bit_length() - 1
ROPE_THETA = 10000.0
EPS = 1e-6
LOG2E = math.log2(math.e)

P_GQ, P_GK, P_GV, P_GG, P_MG = 0, 1024, 2048, 3072, 4096
P_CQ, P_CKV, P_KR, P_G = 5120, 5632, 5888, 6016
P_COLS = 6144

VMEM_LIMIT = 56 * 1024 * 1024


def _cparams(sem):
    return pltpu.CompilerParams(dimension_semantics=sem, vmem_limit_bytes=VMEM_LIMIT)


def _silu(x):
    h = 0.5 * x
    return h + h * jnp.tanh(h)


IN_TM, IN_TN = 1024, 1536
IN_NJ = P_COLS // IN_TN
IN_ROPE_ROWS = IN_TM // IN_NJ


def _inproj_kernel(x_ref, gain_ref, w_ref, pos_ref, invf_ref, p_ref, g_ref, cs_ref, xn_ref):
    j = pl.program_id(1)

    @pl.when(j == 0)
    def _():
        xf = x_ref[...]
        ms = jnp.mean(xf * xf, axis=-1, keepdims=True)
        xn_ref[...] = (xf * lax.rsqrt(ms + EPS) * gain_ref[...]).astype(BF)

    acc = jnp.dot(xn_ref[...], w_ref[...], preferred_element_type=F32)
    p_ref[...] = acc.astype(BF)

    rows = pl.ds(pl.multiple_of(j * IN_ROPE_ROWS, IN_ROPE_ROWS), IN_ROPE_ROWS)
    lane = lax.broadcasted_iota(jnp.int32, (1, 128), 1)
    ang = pos_ref[rows, :].astype(F32) * invf_ref[...]
    cs_ref[rows, :] = jnp.cos(ang - jnp.where(lane < ROPE, 0.0, 0.5 * math.pi))

    @pl.when(j == IN_NJ - 1)
    def _():
        g_ref[...] = acc[:, IN_TN - 128:]


def _in_proj(x2, gain, w, pos, invf):
    n = x2.shape[0]
    return pl.pallas_call(
        _inproj_kernel,
        grid=(n // IN_TM, IN_NJ),
        in_specs=[
            pl.BlockSpec((IN_TM, D_MODEL), lambda i, j: (i, 0)),
            pl.BlockSpec((1, D_MODEL), lambda i, j: (0, 0)),
            pl.BlockSpec((D_MODEL, IN_TN), lambda i, j: (0, j)),
            pl.BlockSpec((IN_TM, 1), lambda i, j: (i, 0)),
            pl.BlockSpec((1, 128), lambda i, j: (0, 0)),
        ],
        out_specs=[
            pl.BlockSpec((IN_TM, IN_TN), lambda i, j: (i, j)),
            pl.BlockSpec((IN_TM, 128), lambda i, j: (i, 0)),
            pl.BlockSpec((IN_TM, 128), lambda i, j: (i, 0)),
        ],
        out_shape=[
            jax.ShapeDtypeStruct((n, P_COLS), BF),
            jax.ShapeDtypeStruct((n, 128), F32),
            jax.ShapeDtypeStruct((n, 128), F32),
        ],
        scratch_shapes=[pltpu.VMEM((IN_TM, D_MODEL), BF)],
        compiler_params=_cparams(("parallel", "arbitrary")),
        name="in_proj",
    )(x2, gain, w, pos, invf)


PREP_TM = 512
HEAD_SLAB = 256


def _rms(x, gain):
    ms = jnp.mean(x * x, axis=-1, keepdims=True)
    return x * lax.rsqrt(ms + EPS) * gain


def _mla_prep_kernel(cq_ref, ckv_ref, kr_ref, cs_ref, qag_ref, kvag_ref, wuq_ref, wukv_ref,
                     qg_ref, kg_ref, q_ref, k_ref, v_ref):
    lane = lax.broadcasted_iota(jnp.int32, (1, 128), 1)
    lo64 = lane < ROPE
    cs = cs_ref[...]

    cqn = _rms(cq_ref[...].astype(F32), qag_ref[...]).astype(BF)
    y = jnp.dot(cqn, wuq_ref[...], preferred_element_type=F32)
    qg0 = qg_ref[:, :128]
    qgcs = qg_ref[:, 128:] * cs
    for h in range(MLA_HEADS):
        y0 = y[:, HEAD_SLAB * h: HEAD_SLAB * h + 128]
        y1 = y[:, HEAD_SLAB * h + 128: HEAD_SLAB * (h + 1)]
        ss = jnp.sum(y0 * y0 + jnp.where(lo64, y1 * y1, 0.0), axis=-1, keepdims=True)
        r = lax.rsqrt(ss + QK_DIM * EPS)
        q_ref[:, HEAD_SLAB * h: HEAD_SLAB * h + 128] = (y0 * r * qg0).astype(BF)
        q_ref[:, HEAD_SLAB * h + 128: HEAD_SLAB * (h + 1)] = (y1 * r * qgcs).astype(BF)

    ckvn = _rms(ckv_ref[...].astype(F32), kvag_ref[...]).astype(BF)
    kv = jnp.dot(ckvn, wukv_ref[...], preferred_element_type=F32)
    kr = kr_ref[...].astype(F32)
    ssr = jnp.sum(jnp.where(lo64, kr * kr, 0.0), axis=-1, keepdims=True)
    kg0 = kg_ref[:, :128]
    kg1 = kg_ref[:, 128:]
    zk0 = kr * kg1 * cs
    zks = zk0 + pltpu.roll(zk0, ROPE, axis=1)
    for h in range(MLA_HEADS):
        kn = kv[:, HEAD_SLAB * h: HEAD_SLAB * h + 128]
        ss = jnp.sum(kn * kn, axis=-1, keepdims=True) + ssr
        r = lax.rsqrt(ss + QK_DIM * EPS)
        k_ref[:, HEAD_SLAB * h: HEAD_SLAB * h + 128] = (kn * r * kg0).astype(BF)
        k_ref[:, HEAD_SLAB * h + 128: HEAD_SLAB * (h + 1)] = (zks * r).astype(BF)
        v_ref[:, V_DIM * h: V_DIM * (h + 1)] = kv[:, HEAD_SLAB * h + 128: HEAD_SLAB * (h + 1)].astype(BF)


def _mla_prep(p, cs, qag, kvag, wuq, wukv, qg, kg):
    n = p.shape[0]
    tm = PREP_TM
    full = lambda r, c: pl.BlockSpec((r, c), lambda i: (0, 0))
    return pl.pallas_call(
        _mla_prep_kernel,
        grid=(n // tm,),
        in_specs=[
            pl.BlockSpec((tm, Q_LORA), lambda i: (i, P_CQ // Q_LORA)),
            pl.BlockSpec((tm, KV_LORA), lambda i: (i, P_CKV // KV_LORA)),
            pl.BlockSpec((tm, 128), lambda i: (i, P_KR // 128)),
            pl.BlockSpec((tm, 128), lambda i: (i, 0)),
            full(1, Q_LORA), full(1, KV_LORA),
            full(Q_LORA, MLA_HEADS * HEAD_SLAB), full(KV_LORA, MLA_HEADS * HEAD_SLAB),
            full(1, HEAD_SLAB), full(1, HEAD_SLAB),
        ],
        out_specs=[
            pl.BlockSpec((tm, MLA_HEADS * HEAD_SLAB), lambda i: (i, 0)),
            pl.BlockSpec((tm, MLA_HEADS * HEAD_SLAB), lambda i: (i, 0)),
            pl.BlockSpec((tm, MLA_HEADS * V_DIM), lambda i: (i, 0)),
        ],
        out_shape=[
            jax.ShapeDtypeStruct((n, MLA_HEADS * HEAD_SLAB), BF),
            jax.ShapeDtypeStruct((n, MLA_HEADS * HEAD_SLAB), BF),
            jax.ShapeDtypeStruct((n, MLA_HEADS * V_DIM), BF),
        ],
        compiler_params=_cparams(("parallel",)),
        name="mla_prep",
    )(p, p, p, cs, qag, kvag, wuq, wukv, qg, kg)


ATT_TQ = 512
ATT_TKW = 1024
ATT_HG = 4
ATT_NS = ATT_HG
NEG = -1e30


def _attn_kernel(q_ref, k_ref, v_ref, gate_ref, o_ref, m_ref, acc_ref):
    qi = pl.program_id(2)
    ones_col = {
        tk: jnp.where(lax.broadcasted_iota(jnp.int32, (tk, 128), 1) == 0, 1.0, 0.0).astype(BF)
        for tk in (ATT_TQ, ATT_TKW)}

    def scores(hh, off, tk):
        q = q_ref[:, HEAD_SLAB * hh: HEAD_SLAB * (hh + 1)]
        kj = k_ref[pl.ds(off, tk), HEAD_SLAB * hh: HEAD_SLAB * (hh + 1)]
        return lax.dot_general(q, kj, (((1,), (1,)), ((), ())), preferred_element_type=F32)

    def update(hh, off, tk, s, masked, first):
        vj = v_ref[pl.ds(off, tk), V_DIM * hh: V_DIM * (hh + 1)]
        if masked:
            row = lax.broadcasted_iota(jnp.int32, s.shape, 0) + qi * ATT_TQ
            col = lax.broadcasted_iota(jnp.int32, s.shape, 1) + off
            s = jnp.where(col <= row, s, NEG)
        if first:
            m_new = jnp.broadcast_to(jnp.max(s, axis=-1, keepdims=True), m_ref.shape[1:])
        else:
            m_prev = m_ref[hh]
            m_new = jnp.maximum(m_prev, jnp.max(s, axis=-1, keepdims=True))
            alpha = jnp.exp2(m_prev - m_new)
        p = jnp.exp2((s - jnp.concatenate([m_new] * (tk // 128), axis=1)).astype(BF))
        pv = jnp.dot(p, jnp.concatenate([vj, ones_col[tk]], axis=1), preferred_element_type=F32)
        acc_ref[hh] = pv if first else jnp.concatenate([alpha, alpha], axis=1) * acc_ref[hh] + pv
        m_ref[hh] = m_new

    def run(off, tk, masked, first=False):
        ss = [scores(hh, off, tk) for hh in range(ATT_HG)]
        for hh, s in enumerate(ss):
            update(hh, off, tk, s, masked, first)

    def body(t, carry):
        run(pl.multiple_of(t * ATT_TKW, ATT_TKW), ATT_TKW, False)
        return carry

    assert ATT_TKW == 2 * ATT_TQ
    run(pl.multiple_of(qi * ATT_TQ, ATT_TQ), ATT_TQ, True, first=True)
    lax.fori_loop(0, lax.shift_right_logical(qi, 1), body, 0)

    @pl.when(jnp.bitwise_and(qi, 1) == 1)
    def _():
        run(pl.multiple_of((qi - 1) * ATT_TQ, ATT_TQ), ATT_TQ, False)

    for hh in range(ATT_HG):
        acc = acc_ref[hh]
        l = jnp.sum(acc[:, V_DIM:], axis=-1, keepdims=True)
        g = gate_ref[:, V_DIM * hh: V_DIM * (hh + 1)].astype(F32)
        o_ref[:, V_DIM * hh: V_DIM * (hh + 1)] = (
            acc[:, :V_DIM] * (1.0 / l) * _silu(g)).astype(BF)


def _mla_attn(qp, kp, vp, p, batch, seq):
    n = qp.shape[0]
    nq = seq // ATT_TQ
    hg = ATT_HG
    return pl.pallas_call(
        _attn_kernel,
        grid=(batch, MLA_HEADS // hg, nq),
        in_specs=[
            pl.BlockSpec((ATT_TQ, hg * HEAD_SLAB), lambda b, h, i: (b * nq + i, h)),
            pl.BlockSpec((seq, hg * HEAD_SLAB), lambda b, h, i: (b, h), pipeline_mode=pl.Buffered(1)),
            pl.BlockSpec((seq, hg * V_DIM), lambda b, h, i: (b, h)),
            pl.BlockSpec((ATT_TQ, hg * V_DIM), lambda b, h, i: (b * nq + i, P_MG // (hg * V_DIM) + h)),
        ],
        out_specs=pl.BlockSpec((ATT_TQ, hg * V_DIM), lambda b, h, i: (b * nq + i, h)),
        out_shape=jax.ShapeDtypeStruct((n, MLA_HEADS * V_DIM), BF),
        scratch_shapes=[
            pltpu.VMEM((ATT_NS, ATT_TQ, 128), F32),
            pltpu.VMEM((ATT_NS, ATT_TQ, 2 * V_DIM), F32),
        ],
        compiler_params=_cparams(("parallel", "parallel", "arbitrary")),
        name="mla_attn",
    )(qp, kp, vp, p)


GDN_T = 512
GDN_NB = GDN_T // 128


def _split3(x):
    a = x.astype(BF)
    r1 = x - a.astype(F32)
    b = r1.astype(BF)
    c = (r1 - b.astype(F32)).astype(BF)
    return a, b, c


def _dot_nt(a, b):
    return lax.dot_general(a, b, (((1,), (1,)), ((), ())), preferred_element_type=F32)


def _dot_tn(a, b):
    return lax.dot_general(a, b, (((0,), (0,)), ((), ())), preferred_element_type=F32)


def _gdn_kernel(q_ref, k_ref, v_ref, hq_ref, hk_ref, hv_ref, g_ref, gate_ref,
                cw_ref, arow_ref, dtrow_ref, og_ref, o_ref,
                state_ref):
    t = pl.program_id(1)
    T = GDN_T

    @pl.when(t == 0)
    def _():
        state_ref[...] = jnp.zeros(state_ref.shape, F32)

    first = t == 0

    sr = lax.broadcasted_iota(jnp.int32, (128, CONV_W * 128), 0)
    sc = lax.broadcasted_iota(jnp.int32, (128, CONV_W * 128), 1)
    shift_cat = jnp.where(jnp.bitwise_and(sc, LANES - 1) == sr - (CONV_W - 1) + jnp.right_shift(sc, LOG2_LANES),
                          1.0, 0.0).astype(BF)
    row8 = lax.broadcasted_iota(jnp.int32, (8, 1), 0)

    def conv_silu(x_ref, h_ref, col0):
        w = [cw_ref[j:j + 1, col0:col0 + GDN_W] for j in range(CONV_W)]
        wb = [wj.astype(BF) for wj in w]
        hz = jnp.where(first, 0.0, h_ref[...].astype(F32))
        ys = []
        for blk in range(GDN_NB):
            xb = x_ref[128 * blk:128 * (blk + 1), :]
            taps = jnp.concatenate([xb * wb[j] for j in range(CONV_W)], axis=0)
            y = jnp.dot(shift_cat, taps, preferred_element_type=F32)
            corr = jnp.zeros((8, GDN_W), F32)
            for j in range(CONV_W - 1):
                sh = CONV_W - 1 - j
                corr = corr + jnp.where(row8 < sh, pltpu.roll(hz, sh, axis=0), 0.0) * w[j]
            ys += [y[:8] + corr, y[8:]]
            hz = xb[112:].astype(F32)[8:]
        y = jnp.concatenate(ys, axis=0)
        return _silu(y)

    qc = conv_silu(q_ref, hq_ref, 0)
    kc = conv_silu(k_ref, hk_ref, GDN_W)
    vc = conv_silu(v_ref, hv_ref, 2 * GDN_W)

    gin = g_ref[...]
    gx = gin + dtrow_ref[...]
    softplus = jnp.maximum(gx, 0.0) + jnp.log1p(jnp.exp(-jnp.abs(gx)))
    gdec = -jnp.exp(arow_ref[...]) * softplus
    beta = jax.nn.sigmoid(gin)

    ri = lax.broadcasted_iota(jnp.int32, (T, T), 0)
    ci = lax.broadcasted_iota(jnp.int32, (T, T), 1)
    same_chunk = jnp.right_shift(ri, LOG2_CHUNK) == jnp.right_shift(ci, LOG2_CHUNK)
    tri = jnp.where(same_chunk, jnp.where(ci <= ri, 1.0, 0.0), 0.0).astype(BF)
    g1, g2, g3 = _split3(gdec)
    gc = (jnp.dot(tri, g1, preferred_element_type=F32)
          + jnp.dot(tri, g2, preferred_element_type=F32)
          + jnp.dot(tri, g3, preferred_element_type=F32))

    er = lax.broadcasted_iota(jnp.int32, (128, GDN_W), 0)
    ec = lax.broadcasted_iota(jnp.int32, (128, GDN_W), 1)
    e_g = jnp.where(er == jnp.right_shift(ec, LOG2_LANES), 1.0, 0.0).astype(BF)
    e_b = jnp.where(er == jnp.right_shift(ec, LOG2_LANES) + GDN_HEADS, 1.0, 0.0).astype(BF)
    c1, c2, _ = _split3(gc)
    gcb = (jnp.dot(c1, e_g, preferred_element_type=F32)
           + jnp.dot(c2, e_g, preferred_element_type=F32))
    betab = jnp.dot(beta.astype(BF), e_b, preferred_element_type=F32)
    gct = gc.T

    lane = lax.broadcasted_iota(jnp.int32, (1, 128), 1)
    lm0 = lane < CHUNK
    ii = lax.broadcasted_iota(jnp.int32, (CHUNK, 128), 0)
    jm = jnp.bitwise_and(lax.broadcasted_iota(jnp.int32, (CHUNK, 128), 1), CHUNK - 1)
    low = ii >= jm
    strict = ii > jm
    eye_p = jnp.where(ii == jm, 1.0, 0.0)
    heads = range(GDN_HEADS)
    items = [(blk, h) for blk in range(GDN_NB) for h in heads]
    nit = range(len(items))

    def bd(xp):
        return jnp.concatenate([jnp.where(lm0, xp, 0.0), jnp.where(lm0, 0.0, xp)], axis=0).astype(BF)

    def mm(a, b):
        return jnp.dot(a, b, preferred_element_type=F32)

    qd, rhs, kdt, egl, kq, dec = [], [], [], [], [], []
    for blk, h in items:
        hs = slice(GDN_DIM * h, GDN_DIM * (h + 1))
        bs = slice(128 * blk, 128 * (blk + 1))
        qh = qc[bs, hs]
        kh = kc[bs, hs]
        qss = jnp.broadcast_to(jnp.sum(qh * qh, axis=-1, keepdims=True), qh.shape)
        kss = jnp.broadcast_to(jnp.sum(kh * kh, axis=-1, keepdims=True), kh.shape)
        qh = qh * (lax.rsqrt(qss + EPS) * (GDN_DIM ** -0.5))
        kh = kh * lax.rsqrt(kss + EPS)
        bh = betab[bs, hs]
        gh = gcb[bs, hs]
        egh = jnp.exp(gh)
        kbh = kh * bh
        kt = kh.T
        rowp = gct[h:h + 1, bs]
        lhs = jnp.concatenate(
            [jnp.concatenate([kbh[:CHUNK], kbh[CHUNK:]], axis=1),
             jnp.concatenate([qh[:CHUNK], qh[CHUNK:]], axis=1)], axis=0).astype(BF)
        bdt = jnp.concatenate([jnp.where(lm0, kt, 0.0), jnp.where(lm0, 0.0, kt)], axis=0).astype(BF)
        kq.append(mm(lhs, bdt))
        colp = jnp.where(lm0, gh[:CHUNK], gh[CHUNK:])
        dec.append(jnp.where(low, jnp.exp(jnp.where(low, colp - rowp, 0.0)), 0.0))
        rhs.append(jnp.concatenate([vc[bs, hs] * bh, kbh * egh], axis=1).astype(BF))
        qd.append((qh * egh).astype(BF))
        gl0 = gh[CHUNK - 1:CHUNK, :]
        gl1 = gh[2 * CHUNK - 1:2 * CHUNK, :]
        rf0 = jnp.where(lm0, jnp.exp(jnp.where(lm0, gl0 - rowp, 0.0)), 0.0)
        rf1 = jnp.where(lm0, 0.0, jnp.exp(jnp.where(lm0, 0.0, gl1 - rowp)))
        kdt.append(((kt * rf0).astype(BF), (kt * rf1).astype(BF)))
        egl.append((jnp.exp(gl0), jnp.exp(gl1)))

    mneg = [jnp.where(strict, -(kq[i][:CHUNK] * dec[i]), 0.0) for i in nit]
    attn = [kq[i][CHUNK:] * dec[i] for i in nit]
    pinv = [eye_p + mneg[i] for i in nit]
    mp = [mm(mneg[i].astype(BF), bd(mneg[i])) for i in nit]
    for _ in range(4):
        r = [mm(jnp.concatenate([pinv[i], mp[i]], axis=0).astype(BF), bd(mp[i])) for i in nit]
        pinv = [pinv[i] + r[i][:CHUNK] for i in nit]
        mp = [r[i][CHUNK:] for i in nit]
    pinv = [pinv[i] + mm(pinv[i].astype(BF), bd(mp[i])) for i in nit]
    sol = [mm(bd(pinv[i]), rhs[i]) for i in nit]

    st = [state_ref[h] for h in heads]
    o_parts = [[] for _ in heads]
    zeros_c = jnp.zeros((CHUNK, GDN_DIM), BF)
    for blk in range(GDN_NB):
        for c in range(2):
            rs = slice(CHUNK * c, CHUNK * (c + 1))
            r1 = [mm(jnp.concatenate([sol[blk * GDN_HEADS + h][rs, GDN_DIM:].astype(BF),
                                      qd[blk * GDN_HEADS + h][rs]], axis=0), st[h].astype(BF))
                  for h in heads]
            for h in heads:
                i = blk * GDN_HEADS + h
                vn = (sol[i][rs, :GDN_DIM] - r1[h][:CHUNK]).astype(BF)
                vfull = jnp.concatenate([vn, zeros_c] if c == 0 else [zeros_c, vn], axis=0)
                am = jnp.where(lm0, attn[i], 0.0) if c == 0 else jnp.where(lm0, 0.0, attn[i])
                r2 = mm(jnp.concatenate([am.astype(BF), kdt[i][c]], axis=0), vfull)
                o_parts[h].append(r1[h][CHUNK:] + r2[:CHUNK])
                st[h] = st[h] * egl[i][c] + r2[CHUNK:]
    for h in heads:
        hs = slice(GDN_DIM * h, GDN_DIM * (h + 1))
        state_ref[h] = st[h]
        o_h = _rms(jnp.concatenate(o_parts[h], axis=0), og_ref[...])
        gt = gate_ref[:, hs].astype(F32)
        o_ref[:, hs] = (o_h * _silu(gt)).astype(BF)


def _gdn(p, g, cw, arow, dtrow, og, batch, seq):
    n = p.shape[0]
    T = GDN_T
    nt = seq // T
    tok = lambda col: pl.BlockSpec((T, GDN_W), lambda b, t: (b * nt + t, col // GDN_W))
    halo = lambda col: pl.BlockSpec(
        (8, GDN_W), lambda b, t: (jnp.maximum((b * nt + t) * (T // 8) - 1, 0), col // GDN_W))
    full = lambda r, c: pl.BlockSpec((r, c), lambda b, t: (0, 0))
    return pl.pallas_call(
        _gdn_kernel,
        grid=(batch, nt),
        in_specs=[
            tok(P_GQ), tok(P_GK), tok(P_GV), halo(P_GQ), halo(P_GK), halo(P_GV),
            pl.BlockSpec((T, 128), lambda b, t: (b * nt + t, 0)),
            tok(P_GG),
            full(CONV_W, 3 * GDN_W), full(1, 128), full(1, 128), full(1, GDN_DIM),
        ],
        out_specs=pl.BlockSpec((T, GDN_W), lambda b, t: (b * nt + t, 0)),
        out_shape=jax.ShapeDtypeStruct((n, GDN_W), BF),
        scratch_shapes=[
            pltpu.VMEM((GDN_HEADS, GDN_DIM, GDN_DIM), F32),
        ],
        compiler_params=_cparams(("parallel", "arbitrary")),
        name="gdn",
    )(p, p, p, p, p, p, g, p, cw, arow, dtrow, og)


OUT_TM, OUT_TN = 512, 512


def _outproj_kernel(a_ref, b_ref, wa_ref, wb_ref, x_ref, o_ref):
    a = a_ref[...]
    b = b_ref[...]
    for c in range(D_MODEL // OUT_TN):
        cs = slice(OUT_TN * c, OUT_TN * (c + 1))
        acc = jnp.dot(a, wa_ref[:, cs], preferred_element_type=F32)
        acc = acc + jnp.dot(b, wb_ref[:, cs], preferred_element_type=F32)
        o_ref[:, cs] = x_ref[:, cs] + acc


def _out_proj(a, b, w, x2):
    n = x2.shape[0]
    half = a.shape[1]
    return pl.pallas_call(
        _outproj_kernel,
        grid=(n // OUT_TM,),
        in_specs=[
            pl.BlockSpec((OUT_TM, half), lambda i: (i, 0)),
            pl.BlockSpec((OUT_TM, half), lambda i: (i, 0)),
            pl.BlockSpec((half, D_MODEL), lambda i: (0, 0)),
            pl.BlockSpec((half, D_MODEL), lambda i: (1, 0)),
            pl.BlockSpec((OUT_TM, D_MODEL), lambda i: (i, 0)),
        ],
        out_specs=pl.BlockSpec((OUT_TM, D_MODEL), lambda i: (i, 0)),
        out_shape=jax.ShapeDtypeStruct((n, D_MODEL), F32),
        compiler_params=_cparams(("parallel",)),
        name="out_proj",
    )(a, b, w, w, x2)


def _rot_cols(w):
    return jnp.concatenate([-w[..., ROPE // 2:], w[..., :ROPE // 2]], axis=-1)


def _swap_halves(g):
    return jnp.concatenate([g[..., ROPE // 2:], g[..., :ROPE // 2]], axis=-1)


def _prep_w_in(w):
    wb = w.astype(BF)
    cq, ckv, kr, mg, gq, gk, gv, ga, gb, gg = jnp.split(
        wb, [512, 768, 832, 1856, 2880, 3904, 4928, 4936, 4944], axis=1)
    pad = jnp.zeros((w.shape[0], P_COLS - P_G - 16), BF)
    return jnp.concatenate([gq, gk, gv, gg, mg, cq, ckv, kr, _rot_cols(kr), ga, gb, pad], axis=1)


def _prep_w_uq(w):
    w3 = w.reshape(Q_LORA, MLA_HEADS, QK_DIM)
    rope = w3[..., NOPE:]
    return jnp.concatenate([w3, _rot_cols(rope)], axis=-1).reshape(Q_LORA, MLA_HEADS * HEAD_SLAB).astype(BF)


def _prep_qk_gain(g):
    rope = g[NOPE:]
    return jnp.concatenate([g, _swap_halves(rope)])[None, :]


def kernel(x, positions, norm_gain, w_in, mla_q_a_gain, mla_kv_a_gain, w_uq, w_ukv,
           mla_q_norm_gain, mla_k_norm_gain, gdn_conv_w, gdn_a_log, gdn_dt_bias,
           gdn_out_norm_gain, w_out):
    batch, seq, _ = x.shape
    n = batch * seq
    x2 = x.reshape(n, D_MODEL)
    pos = positions.reshape(n, 1)
    half = ROPE // 2
    inv_freq = jnp.power(ROPE_THETA, -jnp.arange(half, dtype=F32) / half)
    invf = jnp.tile(inv_freq, 4)[None, :]

    h = x2
    for layer in range(w_in.shape[0]):
        p, g, cs = _in_proj(h, norm_gain[layer][None, :], _prep_w_in(w_in[layer]), pos, invf)
        qscale = QK_DIM ** -0.5 * LOG2E
        qp, kp, vp = _mla_prep(
            p, cs, mla_q_a_gain[layer][None, :], mla_kv_a_gain[layer][None, :],
            _prep_w_uq(w_uq[layer]), w_ukv[layer].astype(BF),
            _prep_qk_gain(mla_q_norm_gain[layer]) * (qscale * QK_DIM ** 0.5),
            _prep_qk_gain(mla_k_norm_gain[layer]) * QK_DIM ** 0.5)
        o_mla = _mla_attn(qp, kp, vp, p, batch, seq)
        zpad = jnp.zeros((128 - GDN_HEADS,), F32)
        arow = jnp.concatenate([gdn_a_log[layer].astype(F32), zpad])[None, :]
        dtrow = jnp.concatenate([gdn_dt_bias[layer].astype(F32), zpad])[None, :]
        o_gdn = _gdn(p, g, gdn_conv_w[layer], arow, dtrow, gdn_out_norm_gain[layer][None, :], batch, seq)
        h = _out_proj(o_mla, o_gdn, w_out[layer].astype(BF), h)
    return h.reshape(batch, seq, D_MODEL)
```
